```python
import math, functools
import jax
import jax.numpy as jnp
from jax import lax
import numpy as np

D_MODEL = 2048
BATCH = 2
SEQ = 4096
DEPTH = 4
DEC_BATCH = 8
DEC_SEQ = 4
PAST_LEN = 16384
PAGE_SIZE = 128

D_FF = 5504
NORM_EPS = 1e-6
NEG_INF = -1e30
RWKV_WIDTH = 1024
RWKV_HEAD = 64
RWKV_HEADS = RWKV_WIDTH // RWKV_HEAD
DECAY_LORA = 64
AAA_LORA = 64
GATE_LORA = 160
RWKV_COLS = 3 * RWKV_WIDTH + DECAY_LORA + AAA_LORA + GATE_LORA
LN_X_EPS = 64e-5
MOBA_WIDTH = 1024
MOBA_HEAD = 128
MOBA_HEADS = MOBA_WIDTH // MOBA_HEAD
MOBA_BLOCK = 256
MOBA_TOPK = 3
MOBA_Q_CHUNK = 32
MOBA_COLS = 3 * MOBA_WIDTH
SSM_WIDTH = 1024
SSM_HEAD = 64
SSM_HEADS = SSM_WIDTH // SSM_HEAD
SSM_GROUPS = 4
SSM_STATE = 128
CONV_WIDTH = 4
SSM_CONV_DIM = SSM_WIDTH + 2 * SSM_GROUPS * SSM_STATE
SSM_COLS = SSM_WIDTH + SSM_CONV_DIM + SSM_HEADS
SSD_CHUNK = 128
N_BRANCH = 3
GATE_COLS = N_BRANCH * D_MODEL
IN_COLS = RWKV_COLS + MOBA_COLS + SSM_COLS + GATE_COLS
MEM_TOKENS = 256
MEM_HEADS = 4
MEM_HEAD = D_MODEL // MEM_HEADS

kernel_name = 'hybrid_rwkv7_moba_ssd_decode_step'


def split_cols(x, sizes):
    cuts = [int(c) for c in np.cumsum(sizes)[:-1]]
    return jnp.split(x, cuts, axis=-1)


def rmsnorm(x, gain):
    xf = x.astype(jnp.float32)
    y = xf * lax.rsqrt(jnp.mean(xf * xf, axis=-1, keepdims=True) + NORM_EPS)
    return (y * gain.astype(jnp.float32)).astype(x.dtype)


def swiglu(x, w_in, w_out):
    gate, up = jnp.split(x @ w_in, 2, axis=-1)
    return (jax.nn.silu(gate) * up) @ w_out


def masked_softmax_stats(s, mask, axes):
    s = jnp.where(mask, s, NEG_INF)
    m = jnp.max(s, axis=axes, keepdims=True)
    p = jnp.where(mask, jnp.exp(s - m), 0.0)
    return m, jnp.sum(p, axis=axes, keepdims=True), p


def merge_partials(m1, l1, a1, m2, l2, a2):
    m = jnp.maximum(m1, m2)
    c1, c2 = jnp.exp(m1 - m), jnp.exp(m2 - m)
    return (a1 * c1[..., None] + a2 * c2[..., None]) / (l1 * c1 + l2 * c2)[..., None]


def rwkv7_recurrence(r, w, k, v, kk, a, s0):
    f32 = jnp.float32
    decay = jnp.exp(-jnp.exp(w.astype(f32)))
    kkf = kk.astype(f32)
    xs = tuple(jnp.moveaxis(t.astype(f32), 1, 0) for t in (r, decay, k, v, kkf, kkf * a.astype(f32)))

    def step(s, inp):
        r_t, d_t, k_t, v_t, kk_t, b_t = inp
        sa = jnp.einsum('bhij,bhj->bhi', s, -kk_t)
        s = s * d_t[:, :, None, :] + sa[..., None] * b_t[:, :, None, :] + v_t[..., None] * k_t[:, :, None, :]
        return s, jnp.einsum('bhij,bhj->bhi', s, r_t)

    s_fin, ys = lax.scan(step, s0.astype(f32), xs)
    return jnp.moveaxis(ys, 0, 1), s_fin


def rwkv7_branch(z, shift0, s0, p):
    b, t, _ = z.shape
    z_prev = jnp.concatenate([shift0[:, None, :], z[:, :-1]], axis=1)
    zm = z + (z_prev - z) * p['rw_mu']
    r, k, v, w_lo, a_lo, g_lo = split_cols(zm, [RWKV_WIDTH, RWKV_WIDTH, RWKV_WIDTH, DECAY_LORA, AAA_LORA, GATE_LORA])
    w = -jax.nn.softplus(-(p['rw_w0'] + jnp.tanh(w_lo) @ p['rw_w_up'])) - 0.5
    a = jax.nn.sigmoid(p['rw_a0'] + a_lo @ p['rw_a_up'])
    g = jax.nn.sigmoid(g_lo) @ p['rw_g_up']

    def heads(u):
        return u.reshape(b, t, RWKV_HEADS, RWKV_HEAD)

    kk = heads(k * p['rw_k_k']).astype(jnp.float32)
    kk = kk * lax.rsqrt(jnp.maximum(jnp.sum(kk * kk, axis=-1, keepdims=True), 1e-12))
    k = k * (1.0 + (a - 1.0) * p['rw_k_a'])
    rh, kh, vh = heads(r), heads(k), heads(v)
    y, s_new = rwkv7_recurrence(rh, heads(w), kh, vh, kk, heads(a), s0)
    mu = jnp.mean(y, axis=-1, keepdims=True)
    var = jnp.mean(jnp.square(y - mu), axis=-1, keepdims=True)
    y = ((y - mu) * lax.rsqrt(var + LN_X_EPS)).reshape(b, t, RWKV_WIDTH)
    y = (y * p['rw_ln_g'].astype(jnp.float32) + p['rw_ln_b'].astype(jnp.float32)).astype(z.dtype)
    bonus = jnp.sum(rh * kh * p['rw_r_k'], axis=-1, keepdims=True) * vh
    out = (y + bonus.reshape(b, t, RWKV_WIDTH)) * g
    return out @ p['rw_out'], z[:, -1], s_new.astype(z.dtype)


def moba_prompt(q, k, v):
    b, t, nh, hd = q.shape
    f32 = jnp.float32
    scale = hd ** -0.5
    nbp = -(-t // MOBA_BLOCK)
    pad = nbp * MOBA_BLOCK - t

    def blocks(u):
        u = jnp.pad(u.astype(f32), ((0, 0), (0, pad), (0, 0), (0, 0)))
        return u.reshape(b, nbp, MOBA_BLOCK, nh, hd)

    qb, kb, vb = blocks(q), blocks(k), blocks(v)
    causal = jnp.tril(jnp.ones((MOBA_BLOCK, MOBA_BLOCK), bool))
    s = jnp.einsum('bnqhd,bnkhd->bnhqk', qb, kb) * scale
    m2, l2, p2 = masked_softmax_stats(s, causal, (-1,))
    a2 = jnp.einsum('bnhqk,bnkhd->bnqhd', p2, vb).reshape(b, nbp * MOBA_BLOCK, nh, hd)[:, :t]
    m2 = m2[..., 0].transpose(0, 1, 3, 2).reshape(b, nbp * MOBA_BLOCK, nh)[:, :t]
    l2 = l2[..., 0].transpose(0, 1, 3, 2).reshape(b, nbp * MOBA_BLOCK, nh)[:, :t]
    nbf = t // MOBA_BLOCK
    ksel = min(MOBA_TOPK, nbf)
    if ksel == 0:
        return a2 / l2[..., None]
    kf = k.astype(f32)[:, :nbf * MOBA_BLOCK].reshape(b, nbf, MOBA_BLOCK, nh, hd)
    vf = v.astype(f32)[:, :nbf * MOBA_BLOCK].reshape(b, nbf, MOBA_BLOCK, nh, hd)
    kmean = jnp.mean(kf, axis=2)
    qf = q.astype(f32)
    s_blk = jnp.einsum('bthd,bnhd->bthn', qf, kmean)
    qblk = jnp.arange(t) // MOBA_BLOCK
    past_ok = jnp.arange(nbf)[None, :] < qblk[:, None]
    s_blk = jnp.where(past_ok[None, :, None, :], s_blk, NEG_INF)
    _, idx = lax.top_k(s_blk, ksel)
    sel_ok = jnp.arange(ksel)[None, :] < qblk[:, None]
    kbh = kf.transpose(0, 3, 1, 2, 4)
    vbh = vf.transpose(0, 3, 1, 2, 4)
    bi = jnp.arange(b)[:, None, None, None]
    hi = jnp.arange(nh)[None, None, :, None]
    nq = t // MOBA_Q_CHUNK

    def attend_chunk(args):
        qc, ic, okc = args
        kg = kbh[bi, hi, ic]
        vg = vbh[bi, hi, ic]
        sc = jnp.einsum('bqhd,bqhskd->bqhsk', qc, kg) * scale
        m, l, pr = masked_softmax_stats(sc, okc[None, :, None, :, None], (-2, -1))
        return m[..., 0, 0], l[..., 0, 0], jnp.einsum('bqhsk,bqhskd->bqhd', pr, vg)

    xs = (qf.reshape(b, nq, MOBA_Q_CHUNK, nh, hd).swapaxes(0, 1),
          idx.reshape(b, nq, MOBA_Q_CHUNK, nh, ksel).swapaxes(0, 1),
          sel_ok.reshape(nq, MOBA_Q_CHUNK, ksel))
    m1, l1, a1 = lax.map(attend_chunk, xs)
    m1 = m1.swapaxes(0, 1).reshape(b, t, nh)
    l1 = l1.swapaxes(0, 1).reshape(b, t, nh)
    a1 = a1.swapaxes(0, 1).reshape(b, t, nh, hd)
    return merge_partials(m1, l1, a1, m2, l2, a2)


def moba_sample(q, k, v, pool_k, pool_v, page_table):
    db, s_new, nh, hd = q.shape
    f32 = jnp.float32
    scale = hd ** -0.5
    past = page_table.shape[1] * PAGE_SIZE
    ppb = MOBA_BLOCK // PAGE_SIZE
    nbf = past // MOBA_BLOCK
    own_past = past - nbf * MOBA_BLOCK
    qf, kf, vf = q.astype(f32), k.astype(f32), v.astype(f32)
    if own_past > 0:
        own_phys = page_table[:, nbf * ppb:]
        k_own = jnp.concatenate([pool_k[own_phys].reshape(db, own_past, nh, hd).astype(f32), kf], axis=1)
        v_own = jnp.concatenate([pool_v[own_phys].reshape(db, own_past, nh, hd).astype(f32), vf], axis=1)
    else:
        k_own, v_own = kf, vf
    kpos = jnp.arange(own_past + s_new)
    qpos = own_past + jnp.arange(s_new)
    s = jnp.einsum('bqhd,bkhd->bhqk', qf, k_own) * scale
    m2, l2, p2 = masked_softmax_stats(s, kpos[None, :] <= qpos[:, None], (-1,))
    a2 = jnp.einsum('bhqk,bkhd->bqhd', p2, v_own)
    m2 = m2[..., 0].transpose(0, 2, 1)
    l2 = l2[..., 0].transpose(0, 2, 1)
    ksel = min(MOBA_TOPK, nbf)
    if ksel == 0:
        return a2 / l2[..., None]
    k_past = pool_k[page_table[:, :nbf * ppb]].astype(f32)
    kmean = jnp.mean(k_past.reshape(db, nbf, MOBA_BLOCK, nh, hd), axis=2)
    s_blk = jnp.einsum('bqhd,bnhd->bqhn', qf, kmean)
    _, idx = lax.top_k(s_blk, ksel)
    lp = idx[..., None] * ppb + jnp.arange(ppb)
    phys = page_table[jnp.arange(db)[:, None, None, None, None], lp]
    hi = jnp.arange(nh)[None, None, :, None, None, None]
    rows = jnp.arange(PAGE_SIZE)
    kg = pool_k[phys[..., None], rows, hi].reshape(db, s_new, nh, ksel, MOBA_BLOCK, hd).astype(f32)
    vg = pool_v[phys[..., None], rows, hi].reshape(db, s_new, nh, ksel, MOBA_BLOCK, hd).astype(f32)
    sc = jnp.einsum('bqhd,bqhskd->bqhsk', qf, kg) * scale
    m1, l1, p1 = masked_softmax_stats(sc, True, (-2, -1))
    a1 = jnp.einsum('bqhsk,bqhskd->bqhd', p1, vg)
    return merge_partials(m1[..., 0, 0], l1[..., 0, 0], a1, m2, l2, a2)


def ssd_chunked(x, dt, a, bm, cm, h0, chunk):
    b, t, nh, hp = x.shape
    g, n = bm.shape[2], bm.shape[3]
    r = nh // g
    nc = t // chunk
    f32 = jnp.float32
    xdt = (x.astype(f32) * dt[..., None]).reshape(b, nc, chunk, g, r, hp)
    la = (dt * a).reshape(b, nc, chunk, g, r)
    bc = bm.astype(f32).reshape(b, nc, chunk, g, n)
    cc = cm.astype(f32).reshape(b, nc, chunk, g, n)
    cs = jnp.cumsum(la, axis=2)
    causal = jnp.tril(jnp.ones((chunk, chunk), bool))[None, None, :, :, None, None]
    seg = cs[:, :, :, None] - cs[:, :, None, :]
    decay_ls = jnp.exp(jnp.where(causal, seg, -jnp.inf))
    cb = jnp.einsum('bclgn,bcsgn->bclsg', cc, bc)
    y_diag = jnp.einsum('bclsg,bclsgr,bcsgrp->bclgrp', cb, decay_ls, xdt)
    decay_to_end = jnp.exp(cs[:, :, -1:] - cs)
    chunk_states = jnp.einsum('bclgn,bclgr,bclgrp->bcgrpn', bc, decay_to_end, xdt)
    chunk_decay = jnp.exp(cs[:, :, -1])

    def step(h, inp):
        dec, st = inp
        return h * dec[..., None, None] + st, h

    h_fin, h_enter = lax.scan(step, h0.astype(f32).reshape(b, g, r, hp, n),
                              (jnp.moveaxis(chunk_decay, 1, 0), jnp.moveaxis(chunk_states, 1, 0)))
    h_enter = jnp.moveaxis(h_enter, 0, 1)
    y_off = jnp.einsum('bclgn,bcgrpn,bclgr->bclgrp', cc, h_enter, jnp.exp(cs))
    return (y_diag + y_off).reshape(b, t, nh, hp), h_fin.reshape(b, nh, hp, n)


def ssd_branch(zc, conv0, h0, p, chunk):
    b, t, _ = zc.shape
    f32 = jnp.float32
    z, xbc, dt_raw = split_cols(zc, [SSM_WIDTH, SSM_CONV_DIM, SSM_HEADS])
    xpad = jnp.concatenate([conv0, xbc], axis=1)
    conv = lax.conv_general_dilated(xpad, p['conv_w'][:, None, :], (1,), 'VALID',
                                    dimension_numbers=('NWC', 'WIO', 'NWC'),
                                    feature_group_count=SSM_CONV_DIM)
    xbc = jax.nn.silu(conv + p['conv_b'])
    xs, bm, cm = split_cols(xbc, [SSM_WIDTH, SSM_GROUPS * SSM_STATE, SSM_GROUPS * SSM_STATE])
    dt = jax.nn.softplus(dt_raw.astype(f32) + p['dt_bias'].astype(f32))
    a = -jnp.exp(p['a_log'].astype(f32))
    xh = xs.reshape(b, t, SSM_HEADS, SSM_HEAD)
    y, h_new = ssd_chunked(xh, dt, a, bm.reshape(b, t, SSM_GROUPS, SSM_STATE),
                           cm.reshape(b, t, SSM_GROUPS, SSM_STATE), h0, chunk)
    y = y + p['d_skip'].astype(f32)[:, None] * xh.astype(f32)
    yg = (y.reshape(b, t, SSM_WIDTH) * jax.nn.silu(z.astype(f32))).reshape(b, t, SSM_GROUPS, -1)
    yg = yg * lax.rsqrt(jnp.mean(yg * yg, axis=-1, keepdims=True) + NORM_EPS)
    yg = (yg.reshape(b, t, SSM_WIDTH) * p['ssm_norm'].astype(f32)).astype(zc.dtype)
    return yg @ p['ssm_out'], xpad[:, -(CONV_WIDTH - 1):], h_new.astype(zc.dtype)


def mem_kv(mem, p):
    b, nm, _ = mem.shape
    mk, mv = jnp.split(rmsnorm(mem, p['norm_mem']) @ p['w_ckv'], 2, axis=-1)
    return mk.reshape(b, nm, MEM_HEADS, MEM_HEAD), mv.reshape(b, nm, MEM_HEADS, MEM_HEAD)


def mem_cross_attention(xn, mk, mv, p):
    b, t, _ = xn.shape
    f32 = jnp.float32
    q = (xn @ p['w_cq']).reshape(b, t, MEM_HEADS, MEM_HEAD).astype(f32)
    s = jnp.einsum('bqhd,bkhd->bhqk', q, mk.astype(f32)) * (MEM_HEAD ** -0.5)
    pr = jax.nn.softmax(s, axis=-1)
    o = jnp.einsum('bhqk,bkhd->bqhd', pr, mv.astype(f32)).reshape(b, t, D_MODEL).astype(xn.dtype)
    return o @ p['w_co']


def trunk_layer(x, p, rw_shift0, rw_s0, conv0, ssm_h0, moba_fn, mem_k, mem_v, ssd_chunk):
    b, t, _ = x.shape
    h = x + 0.5 * swiglu(rmsnorm(x, p['norm_ffn1']), p['w_ffn1_in'], p['w_ffn1_out'])
    u = rmsnorm(h, p['norm_mix'])
    z_rw, z_moba, z_ssm, z_gate = split_cols(u @ p['w_in'], [RWKV_COLS, MOBA_COLS, SSM_COLS, GATE_COLS])
    y_rw, rw_shift, rw_s = rwkv7_branch(z_rw, rw_shift0, rw_s0, p)
    q, k, v = [c.reshape(b, t, MOBA_HEADS, MOBA_HEAD) for c in split_cols(z_moba, [MOBA_WIDTH] * 3)]
    y_moba = moba_fn(q, k, v).reshape(b, t, MOBA_WIDTH).astype(x.dtype) @ p['moba_out']
    y_ssm, conv_new, ssm_h = ssd_branch(z_ssm, conv0, ssm_h0, p, ssd_chunk)
    gate = jax.nn.sigmoid(z_gate + p['b_gate']).reshape(b, t, N_BRANCH, D_MODEL)
    merged = gate[:, :, 0] * y_rw + gate[:, :, 1] * y_moba + gate[:, :, 2] * y_ssm
    h = h + merged @ p['w_mix_out']
    h = h + mem_cross_attention(rmsnorm(h, p['norm_cross']), mem_k, mem_v, p)
    h = h + 0.5 * swiglu(rmsnorm(h, p['norm_ffn2']), p['w_ffn2_in'], p['w_ffn2_out'])
    return h, (k, v, rw_s, rw_shift, conv_new, ssm_h)


def setup_inputs(seed: int = 0) -> dict:
    key = jax.random.key(seed)
    keys = list(jax.random.split(key, 64))
    f32 = jnp.float32

    def normal(shape, scale=1.0):
        return scale * jax.random.normal(keys.pop(), shape, f32)

    def uniform(shape, lo, hi):
        return jax.random.uniform(keys.pop(), shape, f32, lo, hi)

    def gain(shape):
        return 1.0 + 0.05 * jax.random.normal(keys.pop(), shape, f32)

    n_pages = PAST_LEN // PAGE_SIZE
    n_used = DEC_BATCH * n_pages
    n_pool = n_used + -(-n_used // 4)
    page_table = jax.random.permutation(keys.pop(), n_pool)[:n_used].reshape(DEC_BATCH, n_pages).astype(jnp.int32)
    L, D, F = DEPTH, D_MODEL, D_FF
    dt0 = jnp.exp(uniform((L, SSM_HEADS), math.log(1e-3), math.log(1e-1)))
    return {
        'x_prompt': normal((BATCH, SEQ, D)),
        'x_sample': normal((DEC_BATCH, DEC_SEQ, D)),
        'cache_k': normal((L, n_pool, PAGE_SIZE, MOBA_HEADS, MOBA_HEAD)),
        'cache_v': normal((L, n_pool, PAGE_SIZE, MOBA_HEADS, MOBA_HEAD)),
        'cache_mem_k': normal((L, DEC_BATCH, MEM_TOKENS, MEM_HEADS, MEM_HEAD)),
        'cache_mem_v': normal((L, DEC_BATCH, MEM_TOKENS, MEM_HEADS, MEM_HEAD)),
        'state_rwkv': normal((L, DEC_BATCH, RWKV_HEADS, RWKV_HEAD, RWKV_HEAD), 0.5),
        'state_rwkv_shift': normal((L, DEC_BATCH, RWKV_COLS)),
        'state_conv': normal((L, DEC_BATCH, CONV_WIDTH - 1, SSM_CONV_DIM)),
        'state_ssm': normal((L, DEC_BATCH, SSM_HEADS, SSM_HEAD, SSM_STATE), 0.5),
        'page_table': page_table,
        'mem_prompt': normal((BATCH, MEM_TOKENS, D)),
        'norm_ffn1': gain((L, D)),
        'w_ffn1_in': normal((L, D, 2 * F), D ** -0.5),
        'w_ffn1_out': normal((L, F, D), F ** -0.5),
        'norm_mix': gain((L, D)),
        'w_in': normal((L, D, IN_COLS), D ** -0.5),
        'rw_mu': uniform((L, RWKV_COLS), 0.0, 1.0),
        'rw_w0': uniform((L, RWKV_WIDTH), -6.0, 2.0),
        'rw_w_up': normal((L, DECAY_LORA, RWKV_WIDTH), 0.5 * DECAY_LORA ** -0.5),
        'rw_a0': normal((L, RWKV_WIDTH), 0.5),
        'rw_a_up': normal((L, AAA_LORA, RWKV_WIDTH), 0.5 * AAA_LORA ** -0.5),
        'rw_g_up': normal((L, GATE_LORA, RWKV_WIDTH), GATE_LORA ** -0.5),
        'rw_k_k': 0.85 + normal((L, RWKV_WIDTH), 0.05),
        'rw_k_a': gain((L, RWKV_WIDTH)),
        'rw_r_k': normal((L, RWKV_HEADS, RWKV_HEAD), 0.1),
        'rw_ln_g': gain((L, RWKV_WIDTH)),
        'rw_ln_b': normal((L, RWKV_WIDTH), 0.02),
        'rw_out': normal((L, RWKV_WIDTH, D), RWKV_WIDTH ** -0.5),
        'moba_out': normal((L, MOBA_WIDTH, D), MOBA_WIDTH ** -0.5),
        'conv_w': normal((L, CONV_WIDTH, SSM_CONV_DIM), CONV_WIDTH ** -0.5),
        'conv_b': normal((L, SSM_CONV_DIM), 0.02),
        'dt_bias': dt0 + jnp.log(-jnp.expm1(-dt0)),
        'a_log': jnp.log(uniform((L, SSM_HEADS), 1.0, 16.0)),
        'd_skip': gain((L, SSM_HEADS)),
        'ssm_norm': gain((L, SSM_WIDTH)),
        'ssm_out': normal((L, SSM_WIDTH, D), SSM_WIDTH ** -0.5),
        'b_gate': normal((L, GATE_COLS), 0.1),
        'w_mix_out': normal((L, D, D), D ** -0.5),
        'norm_cross': gain((L, D)),
        'norm_mem': gain((L, D)),
        'w_cq': normal((L, D, D), D ** -0.5),
        'w_ckv': normal((L, D, 2 * D), D ** -0.5),
        'w_co': normal((L, D, D), D ** -0.5),
        'norm_ffn2': gain((L, D)),
        'w_ffn2_in': normal((L, D, 2 * F), D ** -0.5),
        'w_ffn2_out': normal((L, F, D), F ** -0.5),
        'norm_final': gain((D,)),
    }


def reference(x_prompt, x_sample, cache_k, cache_v, cache_mem_k, cache_mem_v, state_rwkv,
              state_rwkv_shift, state_conv, state_ssm, page_table, mem_prompt,
              norm_ffn1, w_ffn1_in, w_ffn1_out, norm_mix, w_in, rw_mu, rw_w0, rw_w_up, rw_a0,
              rw_a_up, rw_g_up, rw_k_k, rw_k_a, rw_r_k, rw_ln_g, rw_ln_b, rw_out, moba_out,
              conv_w, conv_b, dt_bias, a_log, d_skip, ssm_norm, ssm_out, b_gate, w_mix_out,
              norm_cross, norm_mem, w_cq, w_ckv, w_co, norm_ffn2, w_ffn2_in, w_ffn2_out,
              norm_final):
    dtype = x_prompt.dtype
    b = x_prompt.shape[0]
    h_p, h_s = x_prompt, x_sample
    p_k, p_v, p_mk, p_mv, p_rw, p_sh, p_cv, p_ss = [], [], [], [], [], [], [], []
    s_k, s_v, s_rw, s_sh, s_cv, s_ss = [], [], [], [], [], []
    for l in range(DEPTH):
        p = {'norm_ffn1': norm_ffn1[l], 'w_ffn1_in': w_ffn1_in[l], 'w_ffn1_out': w_ffn1_out[l],
             'norm_mix': norm_mix[l], 'w_in': w_in[l], 'rw_mu': rw_mu[l], 'rw_w0': rw_w0[l],
             'rw_w_up': rw_w_up[l], 'rw_a0': rw_a0[l], 'rw_a_up': rw_a_up[l], 'rw_g_up': rw_g_up[l],
             'rw_k_k': rw_k_k[l], 'rw_k_a': rw_k_a[l], 'rw_r_k': rw_r_k[l], 'rw_ln_g': rw_ln_g[l],
             'rw_ln_b': rw_ln_b[l], 'rw_out': rw_out[l], 'moba_out': moba_out[l],
             'conv_w': conv_w[l], 'conv_b': conv_b[l], 'dt_bias': dt_bias[l], 'a_log': a_log[l],
             'd_skip': d_skip[l], 'ssm_norm': ssm_norm[l], 'ssm_out': ssm_out[l],
             'b_gate': b_gate[l], 'w_mix_out': w_mix_out[l], 'norm_cross': norm_cross[l],
             'norm_mem': norm_mem[l], 'w_cq': w_cq[l], 'w_ckv': w_ckv[l], 'w_co': w_co[l],
             'norm_ffn2': norm_ffn2[l], 'w_ffn2_in': w_ffn2_in[l], 'w_ffn2_out': w_ffn2_out[l]}
        mk_p, mv_p = mem_kv(mem_prompt, p)
        h_p, st = trunk_layer(h_p, p,
                              jnp.zeros((b, RWKV_COLS), dtype),
                              jnp.zeros((b, RWKV_HEADS, RWKV_HEAD, RWKV_HEAD), dtype),
                              jnp.zeros((b, CONV_WIDTH - 1, SSM_CONV_DIM), dtype),
                              jnp.zeros((b, SSM_HEADS, SSM_HEAD, SSM_STATE), dtype),
                              moba_prompt, mk_p, mv_p, SSD_CHUNK)
        p_k.append(st[0]); p_v.append(st[1]); p_rw.append(st[2]); p_sh.append(st[3])
        p_cv.append(st[4]); p_ss.append(st[5]); p_mk.append(mk_p); p_mv.append(mv_p)
        moba_fn = functools.partial(moba_sample, pool_k=cache_k[l], pool_v=cache_v[l], page_table=page_table)
        h_s, st = trunk_layer(h_s, p, state_rwkv_shift[l], state_rwkv[l], state_conv[l], state_ssm[l],
                              moba_fn, cache_mem_k[l], cache_mem_v[l], x_sample.shape[1])
        s_k.append(st[0]); s_v.append(st[1]); s_rw.append(st[2]); s_sh.append(st[3])
        s_cv.append(st[4]); s_ss.append(st[5])
    y_prompt = rmsnorm(h_p, norm_final)
    y_sample = rmsnorm(h_s, norm_final)
    return (y_prompt, y_sample,
            jnp.stack(p_k), jnp.stack(p_v), jnp.stack(p_mk), jnp.stack(p_mv),
            jnp.stack(p_rw), jnp.stack(p_sh), jnp.stack(p_cv), jnp.stack(p_ss),
            jnp.stack(s_k), jnp.stack(s_v), jnp.stack(s_rw), jnp.stack(s_sh),
            jnp.stack(s_cv), jnp.stack(s_ss))
```

```python
import functools

import jax
import jax.numpy as jnp
import numpy as np
from jax import lax
from jax.experimental import pallas as pl
from jax.experimental.pallas import tpu as pltpu

F32 = jnp.float32
BF16 = jnp.bfloat16

D_MODEL = 2048
D_FF = 5504
D_FF_PAD = 5632
NORM_EPS = 1e-6
NEG_INF = -1e30
REMOVED = -3e38

RWKV_WIDTH = 1024
RWKV_HEAD = 64
RWKV_HEADS = 16
DECAY_LORA = 64
AAA_LORA = 64
GATE_LORA = 160
RWKV_COLS = 3 * RWKV_WIDTH + DECAY_LORA + AAA_LORA + GATE_LORA
LN_X_EPS = 64e-5
LORA_PAD = 128
GATE_LORA_PAD = 256
RWKV_COLS_PAD = 3 * RWKV_WIDTH + 2 * LORA_PAD + GATE_LORA_PAD
RWKV_CHUNK = 64
RWKV_INV_BASE = 16

MOBA_WIDTH = 1024
MOBA_HEAD = 128
MOBA_HEADS = 8
MOBA_BLOCK = 256
MOBA_TOPK = 3
PAGE_SIZE = 128

SSM_WIDTH = 1024
SSM_HEAD = 64
SSM_HEADS = 16
SSM_GROUPS = 4
SSM_STATE = 128
CONV_WIDTH = 4
SSM_CONV_DIM = SSM_WIDTH + 2 * SSM_GROUPS * SSM_STATE
SSM_COLS = SSM_WIDTH + SSM_CONV_DIM + SSM_HEADS
SSM_DT_PAD = 128
SSM_COLS_PAD = SSM_WIDTH + SSM_CONV_DIM + SSM_DT_PAD
SSD_CHUNK = 128
CONV_TAIL = 8

N_BRANCH = 3
MEM_HEADS = 4
MEM_HEAD = 512

LANE = 128
VMEM_LIMIT = 56 * 1024 * 1024


def _cparams(sem):
    return pltpu.CompilerParams(dimension_semantics=sem, vmem_limit_bytes=VMEM_LIMIT)


def _dg(a, b, ca, cb):
    return lax.dot_general(a, b, (((ca,), (cb,)), ((), ())), preferred_element_type=F32)


def _mm(a, b, ca=1, cb=0):
    return _dg(a.astype(BF16), b.astype(BF16), ca, cb)


def _split2(x):
    hi = x.astype(BF16)
    lo = (x - hi.astype(F32)).astype(BF16)
    return hi, lo


def _split3(x):
    hi = x.astype(BF16)
    r = x - hi.astype(F32)
    mid = r.astype(BF16)
    lo = (r - mid.astype(F32)).astype(BF16)
    return hi, mid, lo


def _mm3(a, b, ca=1, cb=0):
    ah, al = _split2(a)
    bh, bl = _split2(b)
    return _dg(ah, bh, ca, cb) + (_dg(ah, bl, ca, cb) + _dg(al, bh, ca, cb))


def _mm6(a, b, ca=1, cb=0):
    a0, a1, a2 = _split3(a)
    b0, b1, b2 = _split3(b)
    small = _dg(a0, b2, ca, cb) + _dg(a2, b0, ca, cb) + _dg(a1, b1, ca, cb)
    mid = _dg(a0, b1, ca, cb) + _dg(a1, b0, ca, cb)
    return _dg(a0, b0, ca, cb) + (mid + small)


def _mm_exact_lhs(lhs01, x):
    x0, x1, x2 = _split3(x)
    return _dg(lhs01, x0, 1, 0) + (_dg(lhs01, x1, 1, 0) + _dg(lhs01, x2, 1, 0))


def _mm_exact_rhs(x, rhs01):
    x0, x1, x2 = _split3(x)
    return _dg(x0, rhs01, 1, 0) + (_dg(x1, rhs01, 1, 0) + _dg(x2, rhs01, 1, 0))


def _sigmoid(x):
    return 1.0 / (1.0 + jnp.exp(-x))


def _silu(x):
    return x * _sigmoid(x)


def _softplus(x):
    return jnp.maximum(x, 0.0) + jnp.log(1.0 + jnp.exp(-jnp.abs(x)))


def _rms_rows(x, gain):
    ms = jnp.mean(x * x, axis=-1, keepdims=True)
    return x * lax.rsqrt(ms + NORM_EPS) * gain


def _normmm_kernel(x_ref, g_ref, w_ref, o_ref, xn_ref):
    @pl.when(pl.program_id(1) == 0)
    def _():
        xn_ref[...] = _rms_rows(x_ref[...], g_ref[...]).astype(BF16)

    o_ref[...] = jnp.dot(xn_ref[...], w_ref[...], preferred_element_type=F32).astype(o_ref.dtype)


def _norm_swiglu_kernel(x_ref, g_ref, wg_ref, wu_ref, o_ref, xn_ref):
    @pl.when(pl.program_id(1) == 0)
    def _():
        xn_ref[...] = _rms_rows(x_ref[...], g_ref[...]).astype(BF16)

    xn = xn_ref[...]
    gate = jnp.dot(xn, wg_ref[...], preferred_element_type=F32)
    up = jnp.dot(xn, wu_ref[...], preferred_element_type=F32)
    o_ref[...] = (_silu(gate) * up).astype(o_ref.dtype)


def _row_tile(m, want):
    return want if m % want == 0 else m


def norm_matmul(x, gain, w, *, tn, out_dtype=F32, tm=1024):
    m, k = x.shape
    n = w.shape[1]
    tm = _row_tile(m, tm)
    return pl.pallas_call(
        _normmm_kernel,
        grid=(m // tm, n // tn),
        in_specs=[pl.BlockSpec((tm, k), lambda i, j: (i, 0)),
                  pl.BlockSpec((1, k), lambda i, j: (0, 0)),
                  pl.BlockSpec((k, tn), lambda i, j: (0, j))],
        out_specs=pl.BlockSpec((tm, tn), lambda i, j: (i, j)),
        out_shape=jax.ShapeDtypeStruct((m, n), out_dtype),
        scratch_shapes=[pltpu.VMEM((tm, k), BF16)],
        compiler_params=_cparams(("parallel", "arbitrary")),
        name="norm_matmul",
    )(x, gain, w)


def norm_swiglu(x, gain, wg, wu, *, tn=512, tm=1024):
    m, k = x.shape
    n = wg.shape[1]
    tm = _row_tile(m, tm)
    return pl.pallas_call(
        _norm_swiglu_kernel,
        grid=(m // tm, n // tn),
        in_specs=[pl.BlockSpec((tm, k), lambda i, j: (i, 0)),
                  pl.BlockSpec((1, k), lambda i, j: (0, 0)),
                  pl.BlockSpec((k, tn), lambda i, j: (0, j)),
                  pl.BlockSpec((k, tn), lambda i, j: (0, j))],
        out_specs=pl.BlockSpec((tm, tn), lambda i, j: (i, j)),
        out_shape=jax.ShapeDtypeStruct((m, n), BF16),
        scratch_shapes=[pltpu.VMEM((tm, k), BF16)],
        compiler_params=_cparams(("parallel", "arbitrary")),
        name="norm_swiglu",
    )(x, gain, wg, wu)


def _norm_kernel(x_ref, g_ref, o_ref):
    o_ref[...] = _rms_rows(x_ref[...], g_ref[...])


def rms_norm(x, gain, *, tm=512):
    m, k = x.shape
    tm = _row_tile(m, tm)
    return pl.pallas_call(
        _norm_kernel,
        grid=(m // tm,),
        in_specs=[pl.BlockSpec((tm, k), lambda i: (i, 0)),
                  pl.BlockSpec((1, k), lambda i: (0, 0))],
        out_specs=pl.BlockSpec((tm, k), lambda i: (i, 0)),
        out_shape=jax.ShapeDtypeStruct((m, k), F32),
        compiler_params=_cparams(("parallel",)),
        name="rms_norm",
    )(x, gain)


def _mmres_kernel(a_ref, w_ref, r_ref, o_ref, *, scale):
    y = jnp.dot(a_ref[...], w_ref[...], preferred_element_type=F32)
    o_ref[...] = r_ref[...] + scale * y


def matmul_residual(a, w, res, *, scale, tm=512, tn=1024):
    m, k = a.shape
    n = w.shape[1]
    tm = _row_tile(m, tm)
    return pl.pallas_call(
        functools.partial(_mmres_kernel, scale=scale),
        grid=(n // tn, m // tm),
        in_specs=[pl.BlockSpec((tm, k), lambda j, i: (i, 0)),
                  pl.BlockSpec((k, tn), lambda j, i: (0, j)),
                  pl.BlockSpec((tm, tn), lambda j, i: (i, j))],
        out_specs=pl.BlockSpec((tm, tn), lambda j, i: (i, j)),
        out_shape=jax.ShapeDtypeStruct((m, n), F32),
        compiler_params=_cparams(("parallel", "parallel")),
        name="matmul_residual",
    )(a, w, res)


def _merge_kernel(arw_ref, amo_ref, ass_ref, wrw_ref, wmo_ref, wss_ref,
                  g0_ref, g1_ref, g2_ref, b0_ref, b1_ref, b2_ref, o_ref):
    y_rw = jnp.dot(arw_ref[...], wrw_ref[...], preferred_element_type=F32)
    y_mo = jnp.dot(amo_ref[...], wmo_ref[...], preferred_element_type=F32)
    y_ss = jnp.dot(ass_ref[...], wss_ref[...], preferred_element_type=F32)
    merged = (_sigmoid(g0_ref[...] + b0_ref[...]) * y_rw
              + _sigmoid(g1_ref[...] + b1_ref[...]) * y_mo
              + _sigmoid(g2_ref[...] + b2_ref[...]) * y_ss)
    o_ref[...] = merged.astype(o_ref.dtype)


def gated_merge(a_rw, a_moba, a_ssm, w_rw, w_moba, w_ssm, z_gate, b_gate, *, tm=512, tn=512):
    m, k = a_rw.shape
    n = w_rw.shape[1]
    tm = _row_tile(m, tm)
    nj = n // tn
    a_spec = pl.BlockSpec((tm, k), lambda i, j: (i, 0))
    w_spec = pl.BlockSpec((k, tn), lambda i, j: (0, j))

    def gate_spec(br):
        return pl.BlockSpec((tm, tn), lambda i, j: (i, br * nj + j))

    def bias_spec(br):
        return pl.BlockSpec((1, tn), lambda i, j: (0, br * nj + j))

    return pl.pallas_call(
        _merge_kernel,
        grid=(m // tm, nj),
        in_specs=[a_spec, a_spec, a_spec, w_spec, w_spec, w_spec,
                  gate_spec(0), gate_spec(1), gate_spec(2),
                  bias_spec(0), bias_spec(1), bias_spec(2)],
        out_specs=pl.BlockSpec((tm, tn), lambda i, j: (i, j)),
        out_shape=jax.ShapeDtypeStruct((m, n), BF16),
        compiler_params=_cparams(("parallel", "parallel")),
        name="gated_merge",
    )(a_rw, a_moba, a_ssm, w_rw, w_moba, w_ssm, z_gate, z_gate, z_gate, b_gate, b_gate, b_gate)


def _unit_lower_inverse(low, n):
    row = lax.broadcasted_iota(jnp.int32, (n, n), 0)
    col = lax.broadcasted_iota(jnp.int32, (n, n), 1)
    eye = (row == col).astype(F32)
    size = RWKV_INV_BASE

    def same_block(width):
        sh = width.bit_length() - 1
        return jnp.right_shift(row, sh) == jnp.right_shift(col, sh)

    same = same_block(size)
    ld = jnp.where(same, low, 0.0)
    inv = eye + ld
    power = ld
    span = 2
    while span < size:
        power = _mm3(power, power)
        inv = inv + _mm3(inv, power)
        span *= 2
    while size < n:
        same_next = same_block(2 * size)
        off = jnp.where(same_next & jnp.logical_not(same), low, 0.0)
        inv = inv + _mm3(_mm3(inv, off), inv)
        same = same_next
        size *= 2
    return inv


def _rwkv_kernel(z_ref, sh0_ref, s0_ref, mu_ref, w0_ref, a0_ref, kk_ref, ka_ref, rk_ref,
                 lng_ref, lnb_ref, wup_ref, aup_ref, gup_ref, tri_ref, seg_ref, segt_ref,
                 o_ref, sout_ref, state_ref, prev_ref, y_ref, *, t_valid):
    C = RWKV_CHUNK
    W = RWKV_WIDTH
    c = pl.program_id(1)

    @pl.when(c == 0)
    def _():
        state_ref[...] = s0_ref[...]
        prev_ref[...] = sh0_ref[...]

    z = z_ref[...]
    row = lax.broadcasted_iota(jnp.int32, (C, 1), 0)
    valid = (c * C + row) < t_valid
    z_prev = jnp.where(row == 0, prev_ref[...], pltpu.roll(z, 1, axis=0))
    prev_ref[...] = z[C - 1:C, :]
    zm = z + (z_prev - z) * mu_ref[...]
    zm = jnp.where(valid, zm, 0.0)
    r = zm[:, 0:W]
    k = zm[:, W:2 * W]
    v = zm[:, 2 * W:3 * W]
    w_lo = zm[:, 3 * W:3 * W + LORA_PAD]
    a_lo = zm[:, 3 * W + LORA_PAD:3 * W + 2 * LORA_PAD]
    g_lo = zm[:, 3 * W + 2 * LORA_PAD:]

    w = -_softplus(-(w0_ref[...] + _mm(jnp.tanh(w_lo), wup_ref[...]))) - 0.5
    logd = jnp.where(valid, -jnp.exp(w), 0.0)
    a = _sigmoid(a0_ref[...] + _mm(a_lo, aup_ref[...]))
    g = _mm(_sigmoid(g_lo), gup_ref[...])

    seg = seg_ref[...]
    segt = segt_ref[...]

    def head_sum(x):
        return _mm_exact_rhs(_mm_exact_rhs(x, seg), segt)

    kkr = k * kk_ref[...]
    kk = kkr * lax.rsqrt(jnp.maximum(head_sum(kkr * kkr), 1e-12))
    k2 = k * (1.0 + (a - 1.0) * ka_ref[...])
    b = kk * a
    bonus = head_sum(r * k2 * rk_ref[...]) * v

    cum = _mm_exact_lhs(tri_ref[...], logd)
    cum_end = cum[C - 1:C, :]
    e_neg = jnp.exp(-cum)
    e_end = jnp.exp(cum_end - cum)
    a_t = -kk * jnp.exp(cum - logd)
    r_t = r * jnp.exp(cum)
    b_t = b * e_neg
    k_t = k2 * e_neg
    b_h = b * e_end
    k_h = k2 * e_end
    p_end = jnp.exp(cum_end)

    rr = lax.broadcasted_iota(jnp.int32, (C, C), 0)
    cc = lax.broadcasted_iota(jnp.int32, (C, C), 1)
    strict = rr > cc
    lower = rr >= cc

    for h in range(RWKV_HEADS):
        sl = slice(h * RWKV_HEAD, (h + 1) * RWKV_HEAD)
        s0 = state_ref[h]
        ar = jnp.concatenate([a_t[:, sl], r_t[:, sl]], axis=0)
        bk = jnp.concatenate([b_t[:, sl], k_t[:, sl]], axis=0)
        gram = _mm3(ar, bk, 1, 1)
        a_ab = jnp.where(strict, gram[:C, :C], 0.0)
        a_ak = jnp.where(strict, gram[:C, C:], 0.0)
        r_ab = jnp.where(lower, gram[C:, :C], 0.0)
        r_ak = jnp.where(lower, gram[C:, C:], 0.0)
        ars0 = _mm3(ar, s0, 1, 1)
        vh = v[:, sl]
        u = _mm3(_unit_lower_inverse(a_ab, C), ars0[:C] + _mm3(a_ak, vh))
        uv = jnp.concatenate([u, vh], axis=0)
        y = ars0[C:] + _mm3(jnp.concatenate([r_ab, r_ak], axis=1), uv)
        bkh = jnp.concatenate([b_h[:, sl], k_h[:, sl]], axis=0)
        state_ref[h] = s0 * p_end[:, sl] + _mm3(uv, bkh, 0, 0)
        y_ref[:, sl] = y

    y = y_ref[...]
    inv_n = 1.0 / RWKV_HEAD
    dev = y - head_sum(y) * inv_n
    var = head_sum(dev * dev) * inv_n
    yn = dev * lax.rsqrt(var + LN_X_EPS) * lng_ref[...] + lnb_ref[...]
    o_ref[...] = ((yn + bonus) * g).astype(o_ref.dtype)

    @pl.when(c == pl.num_programs(1) - 1)
    def _():
        sout_ref[...] = state_ref[...]


def rwkv_mix(z, shift0, s0, prm, *, t_valid):
    bsz, t, cols = z.shape
    C = RWKV_CHUNK
    W = RWKV_WIDTH

    def vec(n):
        return pl.BlockSpec((1, n), lambda b, c: (0, 0))

    def mat(r, n):
        return pl.BlockSpec((r, n), lambda b, c: (0, 0))

    return pl.pallas_call(
        functools.partial(_rwkv_kernel, t_valid=t_valid),
        grid=(bsz, t // C),
        in_specs=[pl.BlockSpec((None, C, cols), lambda b, c: (b, c, 0)),
                  pl.BlockSpec((None, 1, cols), lambda b, c: (b, 0, 0)),
                  pl.BlockSpec((None, RWKV_HEADS, RWKV_HEAD, RWKV_HEAD), lambda b, c: (b, 0, 0, 0)),
                  vec(cols), vec(W), vec(W), vec(W), vec(W), vec(W), vec(W), vec(W),
                  mat(LORA_PAD, W), mat(LORA_PAD, W), mat(GATE_LORA_PAD, W),
                  mat(C, C), mat(W, LANE), mat(LANE, W)],
        out_specs=[pl.BlockSpec((None, C, W), lambda b, c: (b, c, 0)),
                   pl.BlockSpec((None, RWKV_HEADS, RWKV_HEAD, RWKV_HEAD), lambda b, c: (b, 0, 0, 0))],
        out_shape=[jax.ShapeDtypeStruct((bsz, t, W), BF16),
                   jax.ShapeDtypeStruct((bsz, RWKV_HEADS, RWKV_HEAD, RWKV_HEAD), F32)],
        scratch_shapes=[pltpu.VMEM((RWKV_HEADS, RWKV_HEAD, RWKV_HEAD), F32),
                        pltpu.VMEM((1, cols), F32),
                        pltpu.VMEM((C, W), F32)],
        compiler_params=_cparams(("parallel", "arbitrary")),
        name="rwkv_mix",
    )(z, shift0, s0, prm['mu'], prm['w0'], prm['a0'], prm['k_k'], prm['k_a'], prm['r_k'],
      prm['ln_g'], prm['ln_b'], prm['w_up'], prm['a_up'], prm['g_up'],
      prm['tri'], prm['seg'], prm['segt'])


def _ssd_kernel(z_ref, conv0_ref, h0_ref, cw_ref, cb_ref, dtb_ref, alog_ref, dsk_ref, nrm_ref,
                tri_ref, o_ref, hout_ref, state_ref, tail_ref, y_ref, *, t_valid):
    Q = SSD_CHUNK
    W = SSM_WIDTH
    c = pl.program_id(1)

    @pl.when(c == 0)
    def _():
        state_ref[...] = h0_ref[...]
        tail_ref[...] = conv0_ref[...]

    zall = z_ref[...]
    zg = zall[:, :W]
    x = zall[:, W:W + SSM_CONV_DIM]
    dt_raw = zall[:, W + SSM_CONV_DIM:]
    tail = tail_ref[...]
    tail_ref[...] = x[Q - CONV_TAIL:, :]

    row = lax.broadcasted_iota(jnp.int32, (Q, 1), 0)
    row_t = lax.broadcasted_iota(jnp.int32, (CONV_TAIL, 1), 0)
    cw = cw_ref[...]
    acc = x * cw[CONV_WIDTH - 1:CONV_WIDTH, :]
    for s in range(1, CONV_WIDTH):
        x_s = pltpu.roll(x, s, axis=0)
        t_s = pltpu.roll(tail, s, axis=0)
        top = jnp.where(row_t < s, t_s, x_s[:CONV_TAIL])
        x_s = jnp.concatenate([top, x_s[CONV_TAIL:]], axis=0)
        acc = acc + x_s * cw[CONV_WIDTH - 1 - s:CONV_WIDTH - s, :]
    xbc = _silu(acc + cb_ref[...])
    xs = xbc[:, :W]
    bm = xbc[:, W:W + SSM_GROUPS * SSM_STATE]
    cm = xbc[:, W + SSM_GROUPS * SSM_STATE:]

    valid = (c * Q + row) < t_valid
    dt = jnp.where(valid, _softplus(dt_raw + dtb_ref[...]), 0.0)
    la = dt * (-jnp.exp(alog_ref[...]))
    cs = _mm_exact_lhs(tri_ref[...], la)
    cs_t = cs.T
    cs_end = cs[Q - 1:Q, :]
    to_end = jnp.exp(cs_end - cs)
    from_start = jnp.exp(cs)
    end_decay = jnp.exp(cs_end)

    rr = lax.broadcasted_iota(jnp.int32, (Q, Q), 0)
    cc = lax.broadcasted_iota(jnp.int32, (Q, Q), 1)
    causal = rr >= cc
    heads_per_group = SSM_HEADS // SSM_GROUPS
    for g in range(SSM_GROUPS):
        gs = slice(g * SSM_STATE, (g + 1) * SSM_STATE)
        b_g = bm[:, gs]
        c_g = cm[:, gs]
        cb = _mm(c_g, b_g, 1, 1)
        for r in range(heads_per_group):
            h = g * heads_per_group + r
            sl = slice(h * SSM_HEAD, (h + 1) * SSM_HEAD)
            seg = cs[:, h:h + 1] - cs_t[h:h + 1, :]
            decay = jnp.exp(jnp.where(causal, seg, NEG_INF))
            xh = xs[:, sl]
            xdt = xh * dt[:, h:h + 1]
            hs = state_ref[h]
            y = (_mm(cb * decay, xdt) + _mm(c_g, hs, 1, 1) * from_start[:, h:h + 1]
                 + dsk_ref[:, sl] * xh)
            state_ref[h] = hs * end_decay[:, h:h + 1] + _mm(xdt * to_end[:, h:h + 1], b_g, 0, 0)
            y_ref[:, sl] = y

    yg = y_ref[...] * _silu(zg)
    group = W // SSM_GROUPS
    parts = []
    for g in range(SSM_GROUPS):
        t = yg[:, g * group:(g + 1) * group]
        parts.append(t * lax.rsqrt(jnp.mean(t * t, axis=-1, keepdims=True) + NORM_EPS))
    o_ref[...] = (jnp.concatenate(parts, axis=1) * nrm_ref[...]).astype(o_ref.dtype)

    @pl.when(c == pl.num_programs(1) - 1)
    def _():
        hout_ref[...] = state_ref[...]


def ssd_mix(z, conv0, h0, prm, *, t_valid):
    bsz, t, cols = z.shape
    Q = SSD_CHUNK
    W = SSM_WIDTH

    def vec(n):
        return pl.BlockSpec((1, n), lambda b, c: (0, 0))

    state_spec = pl.BlockSpec((None, SSM_HEADS, SSM_HEAD, SSM_STATE), lambda b, c: (b, 0, 0, 0))
    return pl.pallas_call(
        functools.partial(_ssd_kernel, t_valid=t_valid),
        grid=(bsz, t // Q),
        in_specs=[pl.BlockSpec((None, Q, cols), lambda b, c: (b, c, 0)),
                  pl.BlockSpec((None, CONV_TAIL, SSM_CONV_DIM), lambda b, c: (b, 0, 0)),
                  state_spec,
                  pl.BlockSpec((CONV_WIDTH, SSM_CONV_DIM), lambda b, c: (0, 0)),
                  vec(SSM_CONV_DIM), vec(SSM_DT_PAD), vec(SSM_DT_PAD), vec(W), vec(W),
                  pl.BlockSpec((Q, Q), lambda b, c: (0, 0))],
        out_specs=[pl.BlockSpec((None, Q, W), lambda b, c: (b, c, 0)), state_spec],
        out_shape=[jax.ShapeDtypeStruct((bsz, t, W), BF16),
                   jax.ShapeDtypeStruct((bsz, SSM_HEADS, SSM_HEAD, SSM_STATE), F32)],
        scratch_shapes=[pltpu.VMEM((SSM_HEADS, SSM_HEAD, SSM_STATE), F32),
                        pltpu.VMEM((CONV_TAIL, SSM_CONV_DIM), F32),
                        pltpu.VMEM((Q, W), F32)],
        compiler_params=_cparams(("parallel", "arbitrary")),
        name="ssd_mix",
    )(z, conv0, h0, prm['conv_w'], prm['conv_b'], prm['dt_bias'], prm['a_log'], prm['d_skip'],
      prm['ssm_norm'], prm['tri'])


def _topk_block_mask(scores, n_allowed):
    col = lax.broadcasted_iota(jnp.int32, scores.shape, 1).astype(F32)
    sel = jnp.zeros(scores.shape, F32)
    for r in range(MOBA_TOPK):
        top = jnp.max(scores, axis=-1, keepdims=True)
        first = jnp.min(jnp.where(scores == top, col, float(scores.shape[1])), axis=-1, keepdims=True)
        pick = col == first
        sel = jnp.where(pick & (n_allowed > r), 1.0, sel)
        scores = jnp.where(pick, REMOVED, scores)
    return sel


def _attend(q, kb, vb, keep, carry, scale):
    m, l, acc = carry
    s = jnp.where(keep, _mm(q, kb, 1, 1) * scale, NEG_INF)
    m_new = jnp.maximum(m, jnp.max(s, axis=-1, keepdims=True))
    alpha = jnp.exp(m - m_new)
    p = jnp.where(keep, jnp.exp(s - m_new), 0.0)
    l = l * alpha + jnp.sum(p, axis=-1, keepdims=True)
    acc = acc * alpha + _mm(p, vb)
    return m_new, l, acc


def _block_column(sel, n):
    col = lax.broadcasted_iota(jnp.int32, sel.shape, 1)
    return jnp.sum(jnp.where(col == n, sel, 0.0), axis=-1, keepdims=True) > 0.0


def _moba_prompt_kernel(q_ref, k_ref, v_ref, o_ref, kmean_ref, *, n_blocks):
    BLK = MOBA_BLOCK
    i = pl.program_id(2)
    scale = MOBA_HEAD ** -0.5

    @pl.when(i == 0)
    def _():
        kmean_ref[...] = jnp.zeros_like(kmean_ref)
        for n in range(n_blocks):
            kmean_ref[n:n + 1, :] = jnp.mean(k_ref[n * BLK:(n + 1) * BLK, :], axis=0, keepdims=True)

    q = q_ref[...]
    s_blk = _mm6(q, kmean_ref[...], 1, 1)
    col = lax.broadcasted_iota(jnp.int32, s_blk.shape, 1)
    sel = _topk_block_mask(jnp.where(col < i, s_blk, NEG_INF), i)

    start = pl.multiple_of(i * BLK, BLK)
    rr = lax.broadcasted_iota(jnp.int32, (BLK, BLK), 0)
    cc = lax.broadcasted_iota(jnp.int32, (BLK, BLK), 1)
    init = (jnp.full((BLK, 1), NEG_INF, F32), jnp.zeros((BLK, 1), F32), jnp.zeros((BLK, MOBA_HEAD), F32))
    carry = _attend(q, k_ref[pl.ds(start, BLK), :], v_ref[pl.ds(start, BLK), :], rr >= cc, init, scale)

    def past_block(n, carry):
        off = pl.multiple_of(n * BLK, BLK)
        return _attend(q, k_ref[pl.ds(off, BLK), :], v_ref[pl.ds(off, BLK), :],
                       _block_column(sel, n), carry, scale)

    _, l, acc = lax.fori_loop(0, i, past_block, carry)
    o_ref[...] = (acc / l).astype(o_ref.dtype)


def moba_prompt(q, k, v):
    bsz, t, _ = q.shape
    n_blocks = t // MOBA_BLOCK
    kv_spec = pl.BlockSpec((None, t, MOBA_HEAD), lambda b, h, i: (b, 0, h))
    q_spec = pl.BlockSpec((None, MOBA_BLOCK, MOBA_HEAD), lambda b, h, i: (b, i, h))
    return pl.pallas_call(
        functools.partial(_moba_prompt_kernel, n_blocks=n_blocks),
        grid=(bsz, MOBA_HEADS, n_blocks),
        in_specs=[q_spec, kv_spec, kv_spec],
        out_specs=q_spec,
        out_shape=jax.ShapeDtypeStruct((bsz, t, MOBA_WIDTH), BF16),
        scratch_shapes=[pltpu.VMEM((LANE, MOBA_HEAD), F32)],
        compiler_params=_cparams(("parallel", "parallel", "arbitrary")),
        name="moba_prompt",
    )(q, k, v)


SAMPLE_ROWS = 8
QROWS = MOBA_HEADS * SAMPLE_ROWS


def _moba_sample_kernel(pt_ref, qbd_ref, knew_ref, vnew_ref, kpage_ref, vpage_ref, o_ref,
                        ksum_ref, sel_ref, m_ref, l_ref, acc_ref, *, n_new, pages_per_block):
    ph = pl.program_id(1)
    p = pl.program_id(2)
    n_pages = pl.num_programs(2)
    scale = MOBA_HEAD ** -0.5
    blk = p // pages_per_block

    @pl.when((ph == 0) & (p == 0))
    def _():
        ksum_ref[...] = jnp.zeros_like(ksum_ref)

    @pl.when(ph == 0)
    def _():
        ksum_ref[pl.ds(blk, 1), :] += jnp.sum(kpage_ref[...], axis=0, keepdims=True)

    @pl.when((ph == 1) & (p == 0))
    def _():
        q = qbd_ref[...]
        kmean = ksum_ref[...] * (1.0 / (pages_per_block * PAGE_SIZE))
        s_blk = _mm6(q, kmean, 1, 1)
        col = lax.broadcasted_iota(jnp.int32, s_blk.shape, 1)
        n_blocks = n_pages // pages_per_block
        sel_ref[...] = _topk_block_mask(jnp.where(col < n_blocks, s_blk, NEG_INF), n_blocks)
        rr = jnp.bitwise_and(lax.broadcasted_iota(jnp.int32, (QROWS, PAGE_SIZE), 0), SAMPLE_ROWS - 1)
        cc = lax.broadcasted_iota(jnp.int32, (QROWS, PAGE_SIZE), 1)
        init = (jnp.full((QROWS, 1), NEG_INF, F32), jnp.zeros((QROWS, 1), F32),
                jnp.zeros((QROWS, MOBA_WIDTH), F32))
        m, l, acc = _attend(q, knew_ref[...], vnew_ref[...], (cc <= rr) & (cc < n_new), init, scale)
        m_ref[...] = m
        l_ref[...] = l
        acc_ref[...] = acc

    @pl.when(ph == 1)
    def _():
        carry = (m_ref[...], l_ref[...], acc_ref[...])
        m, l, acc = _attend(qbd_ref[...], kpage_ref[...], vpage_ref[...],
                            _block_column(sel_ref[...], blk), carry, scale)
        m_ref[...] = m
        l_ref[...] = l
        acc_ref[...] = acc

    @pl.when((ph == 1) & (p == n_pages - 1))
    def _():
        out = acc_ref[...] / l_ref[...]
        for h in range(MOBA_HEADS):
            o_ref[:, h * MOBA_HEAD:(h + 1) * MOBA_HEAD] = out[
                h * SAMPLE_ROWS:(h + 1) * SAMPLE_ROWS, h * MOBA_HEAD:(h + 1) * MOBA_HEAD]


def moba_sample(qbd, k_new, v_new, pool_k, pool_v, page_table, *, n_new):
    bsz, n_pages = page_table.shape
    pages_per_block = MOBA_BLOCK // PAGE_SIZE
    row_spec = pl.BlockSpec((None, SAMPLE_ROWS, MOBA_WIDTH), lambda b, ph, p, pt: (b, 0, 0))
    new_spec = pl.BlockSpec((None, PAGE_SIZE, MOBA_WIDTH), lambda b, ph, p, pt: (b, 0, 0))
    grid_spec = pltpu.PrefetchScalarGridSpec(
        num_scalar_prefetch=1,
        grid=(bsz, 2, n_pages),
        in_specs=[pl.BlockSpec((None, QROWS, MOBA_WIDTH), lambda b, ph, p, pt: (b, 0, 0)),
                  new_spec, new_spec,
                  pl.BlockSpec((None, PAGE_SIZE, MOBA_WIDTH), lambda b, ph, p, pt: (pt[b, p], 0, 0)),
                  pl.BlockSpec((None, PAGE_SIZE, MOBA_WIDTH), lambda b, ph, p, pt: (pt[b, p * ph], 0, 0))],
        out_specs=row_spec,
        scratch_shapes=[pltpu.VMEM((LANE, MOBA_WIDTH), F32),
                        pltpu.VMEM((QROWS, LANE), F32),
                        pltpu.VMEM((QROWS, 1), F32),
                        pltpu.VMEM((QROWS, 1), F32),
                        pltpu.VMEM((QROWS, MOBA_WIDTH), F32)])
    return pl.pallas_call(
        functools.partial(_moba_sample_kernel, n_new=n_new, pages_per_block=pages_per_block),
        grid_spec=grid_spec,
        out_shape=jax.ShapeDtypeStruct((bsz, SAMPLE_ROWS, MOBA_WIDTH), F32),
        compiler_params=_cparams(("parallel", "arbitrary", "arbitrary")),
        name="moba_sample",
    )(page_table, qbd, k_new, v_new, pool_k, pool_v)


def _cross_kernel(q_ref, k_ref, v_ref, o_ref):
    scale = MEM_HEAD ** -0.5
    for h in range(MEM_HEADS):
        sl = slice(h * MEM_HEAD, (h + 1) * MEM_HEAD)
        s = _mm(q_ref[:, sl], k_ref[:, sl], 1, 1) * scale
        e = jnp.exp(s - jnp.max(s, axis=-1, keepdims=True))
        pr = e / jnp.sum(e, axis=-1, keepdims=True)
        o_ref[:, sl] = _mm(pr, v_ref[:, sl]).astype(o_ref.dtype)


def cross_attention(q, mk, mv, *, tq=512):
    bsz, t, d = q.shape
    n_mem = mk.shape[1]
    tq = _row_tile(t, tq)
    kv_spec = pl.BlockSpec((None, n_mem, d), lambda b, i: (b, 0, 0))
    q_spec = pl.BlockSpec((None, tq, d), lambda b, i: (b, i, 0))
    return pl.pallas_call(
        _cross_kernel,
        grid=(bsz, t // tq),
        in_specs=[q_spec, kv_spec, kv_spec],
        out_specs=q_spec,
        out_shape=jax.ShapeDtypeStruct((bsz, t, d), BF16),
        compiler_params=_cparams(("parallel", "parallel")),
        name="cross_attention",
    )(q, mk, mv)


def _pad_cols(x, n):
    return jnp.pad(x, [(0, 0)] * (x.ndim - 1) + [(0, n - x.shape[-1])])


def _pad_rows(x, n):
    return jnp.pad(x, [(0, n - x.shape[0])] + [(0, 0)] * (x.ndim - 1))


def _rwkv_cols_padded(x):
    W = RWKV_WIDTH
    o = 3 * W
    return jnp.concatenate([
        x[..., :o],
        _pad_cols(x[..., o:o + DECAY_LORA], LORA_PAD),
        _pad_cols(x[..., o + DECAY_LORA:o + DECAY_LORA + AAA_LORA], LORA_PAD),
        _pad_cols(x[..., o + DECAY_LORA + AAA_LORA:], GATE_LORA_PAD)], axis=-1)


def _rwkv_cols_unpadded(x):
    o = 3 * RWKV_WIDTH
    return jnp.concatenate([
        x[..., :o],
        x[..., o:o + DECAY_LORA],
        x[..., o + LORA_PAD:o + LORA_PAD + AAA_LORA],
        x[..., o + 2 * LORA_PAD:o + 2 * LORA_PAD + GATE_LORA]], axis=-1)


def _constants():
    tri_rw = jnp.asarray(np.tril(np.ones((RWKV_CHUNK, RWKV_CHUNK), np.float32)), BF16)
    tri_ssd = jnp.asarray(np.tril(np.ones((SSD_CHUNK, SSD_CHUNK), np.float32)), BF16)
    seg = np.zeros((RWKV_WIDTH, LANE), np.float32)
    seg[np.arange(RWKV_WIDTH), np.arange(RWKV_WIDTH) // RWKV_HEAD] = 1.0
    return tri_rw, tri_ssd, jnp.asarray(seg, BF16), jnp.asarray(seg.T, BF16)


def _layer_params(l, P, consts):
    tri_rw, tri_ssd, seg, segt = consts
    row = lambda x: x.reshape(1, -1).astype(F32)
    w_in = P['w_in'][l]
    o_moba = RWKV_COLS
    o_ssm = o_moba + 3 * MOBA_WIDTH
    o_gate = o_ssm + SSM_COLS
    lp = {}
    lp['norm_ffn1'] = row(P['norm_ffn1'][l])
    lp['norm_ffn2'] = row(P['norm_ffn2'][l])
    for name in ('ffn1', 'ffn2'):
        wi = P['w_%s_in' % name][l]
        lp['w_%s_gate' % name] = _pad_cols(wi[:, :D_FF], D_FF_PAD).astype(BF16)
        lp['w_%s_up' % name] = _pad_cols(wi[:, D_FF:], D_FF_PAD).astype(BF16)
        lp['w_%s_out' % name] = _pad_rows(P['w_%s_out' % name][l], D_FF_PAD).astype(BF16)
    lp['norm_mix'] = row(P['norm_mix'][l])
    lp['w_rw'] = _rwkv_cols_padded(w_in[:, :RWKV_COLS]).astype(BF16)
    lp['w_q'] = w_in[:, o_moba:o_moba + MOBA_WIDTH].astype(BF16)
    lp['w_k'] = w_in[:, o_moba + MOBA_WIDTH:o_moba + 2 * MOBA_WIDTH].astype(BF16)
    lp['w_v'] = w_in[:, o_moba + 2 * MOBA_WIDTH:o_ssm].astype(BF16)
    lp['w_ssm'] = _pad_cols(w_in[:, o_ssm:o_gate], SSM_COLS_PAD).astype(BF16)
    lp['w_gate'] = w_in[:, o_gate:].astype(BF16)
    lp['rw'] = {
        'mu': _rwkv_cols_padded(row(P['rw_mu'][l])),
        'w0': row(P['rw_w0'][l]), 'a0': row(P['rw_a0'][l]),
        'k_k': row(P['rw_k_k'][l]), 'k_a': row(P['rw_k_a'][l]), 'r_k': row(P['rw_r_k'][l]),
        'ln_g': row(P['rw_ln_g'][l]), 'ln_b': row(P['rw_ln_b'][l]),
        'w_up': _pad_rows(P['rw_w_up'][l], LORA_PAD).astype(BF16),
        'a_up': _pad_rows(P['rw_a_up'][l], LORA_PAD).astype(BF16),
        'g_up': _pad_rows(P['rw_g_up'][l], GATE_LORA_PAD).astype(BF16),
        'tri': tri_rw, 'seg': seg, 'segt': segt,
    }
    lp['ssm'] = {
        'conv_w': P['conv_w'][l].astype(F32), 'conv_b': row(P['conv_b'][l]),
        'dt_bias': _pad_cols(row(P['dt_bias'][l]), SSM_DT_PAD),
        'a_log': _pad_cols(row(P['a_log'][l]), SSM_DT_PAD),
        'd_skip': row(jnp.repeat(P['d_skip'][l], SSM_HEAD)),
        'ssm_norm': row(P['ssm_norm'][l]),
        'tri': tri_ssd,
    }
    lp['rw_out'] = P['rw_out'][l].astype(BF16)
    lp['moba_out'] = P['moba_out'][l].astype(BF16)
    lp['ssm_out'] = P['ssm_out'][l].astype(BF16)
    lp['b_gate'] = row(P['b_gate'][l])
    lp['w_mix_out'] = P['w_mix_out'][l].astype(BF16)
    lp['norm_cross'] = row(P['norm_cross'][l])
    lp['norm_mem'] = row(P['norm_mem'][l])
    lp['w_cq'] = P['w_cq'][l].astype(BF16)
    lp['w_ckv'] = P['w_ckv'][l].astype(BF16)
    lp['w_co'] = P['w_co'][l].astype(BF16)
    return lp


def _ffn(x, gain, wg, wu, wo):
    act = norm_swiglu(x, gain, wg, wu)
    return matmul_residual(act, wo, x, scale=0.5)


def _pad_time(x, t_pad):
    return jnp.pad(x, ((0, 0), (0, t_pad - x.shape[1]), (0, 0)))


def _trunk_layer(x, lp, bsz, t, rw_shift0, rw_s0, conv0, ssm_h0, moba_fn, mem_k, mem_v):
    m = bsz * t
    h = _ffn(x, lp['norm_ffn1'], lp['w_ffn1_gate'], lp['w_ffn1_up'], lp['w_ffn1_out'])

    z_rw = norm_matmul(h, lp['norm_mix'], lp['w_rw'], tn=512)
    q = norm_matmul(h, lp['norm_mix'], lp['w_q'], tn=512)
    k = norm_matmul(h, lp['norm_mix'], lp['w_k'], tn=512)
    v = norm_matmul(h, lp['norm_mix'], lp['w_v'], tn=512)
    z_ssm = norm_matmul(h, lp['norm_mix'], lp['w_ssm'], tn=640)
    z_gate = norm_matmul(h, lp['norm_mix'], lp['w_gate'], tn=512)

    t_rw = -(-t // RWKV_CHUNK) * RWKV_CHUNK
    z_rw3 = z_rw.reshape(bsz, t, RWKV_COLS_PAD)
    a_rw, rw_s = rwkv_mix(_pad_time(z_rw3, t_rw), _rwkv_cols_padded(rw_shift0)[:, None, :], rw_s0,
                          lp['rw'], t_valid=t)
    a_rw = a_rw[:, :t].reshape(m, RWKV_WIDTH)
    rw_shift = _rwkv_cols_unpadded(z_rw3[:, t - 1])

    a_moba = moba_fn(q.reshape(bsz, t, MOBA_WIDTH), k.reshape(bsz, t, MOBA_WIDTH),
                     v.reshape(bsz, t, MOBA_WIDTH)).reshape(m, MOBA_WIDTH)

    t_ssd = -(-t // SSD_CHUNK) * SSD_CHUNK
    z_ssm3 = z_ssm.reshape(bsz, t, SSM_COLS_PAD)
    conv0_pad = jnp.pad(conv0, ((0, 0), (CONV_TAIL - (CONV_WIDTH - 1), 0), (0, 0)))
    a_ssm, ssm_h = ssd_mix(_pad_time(z_ssm3, t_ssd), conv0_pad, ssm_h0, lp['ssm'], t_valid=t)
    a_ssm = a_ssm[:, :t].reshape(m, SSM_WIDTH)
    keep = min(t, CONV_WIDTH - 1)
    xbc_tail = z_ssm3[:, t - keep:, SSM_WIDTH:SSM_WIDTH + SSM_CONV_DIM]
    conv_new = jnp.concatenate([conv0, xbc_tail], axis=1)[:, -(CONV_WIDTH - 1):]

    merged = gated_merge(a_rw, a_moba, a_ssm, lp['rw_out'], lp['moba_out'], lp['ssm_out'],
                         z_gate, lp['b_gate'])
    h = matmul_residual(merged, lp['w_mix_out'], h, scale=1.0)

    cq = norm_matmul(h, lp['norm_cross'], lp['w_cq'], tn=512)
    t_ca = -(-t // 8) * 8
    o = cross_attention(_pad_time(cq.reshape(bsz, t, D_MODEL), t_ca), mem_k, mem_v)
    o = o[:, :t].reshape(m, D_MODEL)
    h = matmul_residual(o, lp['w_co'], h, scale=1.0)

    h = _ffn(h, lp['norm_ffn2'], lp['w_ffn2_gate'], lp['w_ffn2_up'], lp['w_ffn2_out'])
    k4 = k.reshape(bsz, t, MOBA_HEADS, MOBA_HEAD)
    v4 = v.reshape(bsz, t, MOBA_HEADS, MOBA_HEAD)
    return h, (k4, v4, rw_s, rw_shift, conv_new, ssm_h)


def _moba_sample_fn(q, k, v, *, pool_k, pool_v, page_table):
    bsz, s_new, _ = q.shape
    qp = _pad_time(q, SAMPLE_ROWS).reshape(bsz, SAMPLE_ROWS, MOBA_HEADS, MOBA_HEAD)
    eye = jnp.eye(MOBA_HEADS, dtype=F32)
    qbd = jnp.einsum('bthd,hg->bhtgd', qp, eye).reshape(bsz, QROWS, MOBA_WIDTH)
    out = moba_sample(qbd, _pad_time(k, PAGE_SIZE), _pad_time(v, PAGE_SIZE),
                      pool_k.reshape(pool_k.shape[0], PAGE_SIZE, MOBA_WIDTH),
                      pool_v.reshape(pool_v.shape[0], PAGE_SIZE, MOBA_WIDTH),
                      page_table, n_new=s_new)
    return out[:, :s_new].astype(BF16)


def kernel(x_prompt, x_sample, cache_k, cache_v, cache_mem_k, cache_mem_v, state_rwkv, state_rwkv_shift, state_conv, state_ssm, page_table, mem_prompt, norm_ffn1, w_ffn1_in, w_ffn1_out, norm_mix, w_in, rw_mu, rw_w0, rw_w_up, rw_a0, rw_a_up, rw_g_up, rw_k_k, rw_k_a, rw_r_k, rw_ln_g, rw_ln_b, rw_out, moba_out, conv_w, conv_b, dt_bias, a_log, d_skip, ssm_norm, ssm_out, b_gate, w_mix_out, norm_cross, norm_mem, w_cq, w_ckv, w_co, norm_ffn2, w_ffn2_in, w_ffn2_out, norm_final):
    P = dict(norm_ffn1=norm_ffn1, w_ffn1_in=w_ffn1_in, w_ffn1_out=w_ffn1_out, norm_mix=norm_mix,
             w_in=w_in, rw_mu=rw_mu, rw_w0=rw_w0, rw_w_up=rw_w_up, rw_a0=rw_a0, rw_a_up=rw_a_up,
             rw_g_up=rw_g_up, rw_k_k=rw_k_k, rw_k_a=rw_k_a, rw_r_k=rw_r_k, rw_ln_g=rw_ln_g,
             rw_ln_b=rw_ln_b, rw_out=rw_out, moba_out=moba_out, conv_w=conv_w, conv_b=conv_b,
             dt_bias=dt_bias, a_log=a_log, d_skip=d_skip, ssm_norm=ssm_norm, ssm_out=ssm_out,
             b_gate=b_gate, w_mix_out=w_mix_out, norm_cross=norm_cross, norm_mem=norm_mem,
             w_cq=w_cq, w_ckv=w_ckv, w_co=w_co, norm_ffn2=norm_ffn2, w_ffn2_in=w_ffn2_in,
             w_ffn2_out=w_ffn2_out)
    depth = w_in.shape[0]
    bp, tp, d = x_prompt.shape
    bs, ts, _ = x_sample.shape
    n_mem = mem_prompt.shape[1]
    consts = _constants()
    h_p = x_prompt.reshape(bp * tp, d)
    h_s = x_sample.reshape(bs * ts, d)
    mem2 = mem_prompt.reshape(bp * n_mem, d)
    outs_p = [[] for _ in range(8)]
    outs_s = [[] for _ in range(6)]
    for l in range(depth):
        lp = _layer_params(l, P, consts)
        mkv = norm_matmul(mem2, lp['norm_mem'], lp['w_ckv'], tn=512)
        mk_p = mkv[:, :d].reshape(bp, n_mem, d)
        mv_p = mkv[:, d:].reshape(bp, n_mem, d)
        h_p, st = _trunk_layer(
            h_p, lp, bp, tp,
            jnp.zeros((bp, RWKV_COLS), F32),
            jnp.zeros((bp, RWKV_HEADS, RWKV_HEAD, RWKV_HEAD), F32),
            jnp.zeros((bp, CONV_WIDTH - 1, SSM_CONV_DIM), F32),
            jnp.zeros((bp, SSM_HEADS, SSM_HEAD, SSM_STATE), F32),
            moba_prompt, mk_p, mv_p)
        for dst, val in zip(outs_p, (st[0], st[1], mk_p.reshape(bp, n_mem, MEM_HEADS, MEM_HEAD),
                                     mv_p.reshape(bp, n_mem, MEM_HEADS, MEM_HEAD),
                                     st[2], st[3], st[4], st[5])):
            dst.append(val)
        moba_fn = functools.partial(_moba_sample_fn, pool_k=cache_k[l], pool_v=cache_v[l],
                                    page_table=page_table)
        h_s, st = _trunk_layer(
            h_s, lp, bs, ts, state_rwkv_shift[l], state_rwkv[l], state_conv[l], state_ssm[l],
            moba_fn, cache_mem_k[l].reshape(bs, n_mem, d), cache_mem_v[l].reshape(bs, n_mem, d))
        for dst, val in zip(outs_s, st):
            dst.append(val)
    gain = norm_final.reshape(1, d)
    y_prompt = rms_norm(h_p, gain).reshape(bp, tp, d)
    y_sample = rms_norm(h_s, gain).reshape(bs, ts, d)
    return (y_prompt, y_sample) + tuple(jnp.stack(o) for o in outs_p) + tuple(jnp.stack(o) for o in outs_s)
```

```python
import functools

import jax
import jax.numpy as jnp
import numpy as np
from jax import lax
from jax.experimental import pallas as pl
from jax.experimental.pallas import tpu as pltpu

F32 = jnp.float32
BF16 = jnp.bfloat16

D_MODEL = 2048
NORM_EPS = 1e-6
NEG_INF = -1e30
REMOVED = -3e38

RWKV_WIDTH = 1024
RWKV_HEAD = 64
RWKV_HEADS = 16
DECAY_LORA = 64
AAA_LORA = 64
GATE_LORA = 160
RWKV_COLS = 3 * RWKV_WIDTH + DECAY_LORA + AAA_LORA + GATE_LORA
LN_X_EPS = 64e-5
LORA_PAD = 128
GATE_LORA_PAD = 256
RWKV_COLS_PAD = 3 * RWKV_WIDTH + 2 * LORA_PAD + GATE_LORA_PAD
RWKV_CHUNK = 64
RWKV_INV_BASE = 8

MOBA_WIDTH = 1024
MOBA_HEAD = 128
MOBA_HEADS = 8
MOBA_BLOCK = 256
MOBA_TOPK = 3
MOBA_GROUP = 4
PAGE_SIZE = 128

SSM_WIDTH = 1024
SSM_HEAD = 64
SSM_HEADS = 16
SSM_GROUPS = 4
SSM_STATE = 128
CONV_WIDTH = 4
SSM_CONV_DIM = SSM_WIDTH + 2 * SSM_GROUPS * SSM_STATE
SSM_COLS = SSM_WIDTH + SSM_CONV_DIM + SSM_HEADS
SSM_DT_PAD = 128
SSM_COLS_PAD = SSM_WIDTH + SSM_CONV_DIM + SSM_DT_PAD
SSD_CHUNK = 128
CONV_TAIL = 8

N_BRANCH = 3
MEM_HEADS = 4
MEM_HEAD = 512

LANE = 128
VMEM_LIMIT = 56 * 1024 * 1024


def _cparams(sem):
    return pltpu.CompilerParams(dimension_semantics=sem, vmem_limit_bytes=VMEM_LIMIT)


def _dg(a, b, ca, cb):
    return lax.dot_general(a, b, (((ca,), (cb,)), ((), ())), preferred_element_type=F32)


def _mm(a, b, ca=1, cb=0):
    return _dg(a.astype(BF16), b.astype(BF16), ca, cb)


def _split2(x):
    hi = x.astype(BF16)
    lo = (x - hi.astype(F32)).astype(BF16)
    return hi, lo


def _split3(x):
    hi = x.astype(BF16)
    r = x - hi.astype(F32)
    mid = r.astype(BF16)
    lo = (r - mid.astype(F32)).astype(BF16)
    return hi, mid, lo


def _mm3(a, b, ca=1, cb=0):
    ah, al = _split2(a)
    bh, bl = _split2(b)
    return _dg(ah, bh, ca, cb) + (_dg(ah, bl, ca, cb) + _dg(al, bh, ca, cb))


def _mm6(a, b, ca=1, cb=0):
    a0, a1, a2 = _split3(a)
    b0, b1, b2 = _split3(b)
    small = _dg(a0, b2, ca, cb) + _dg(a2, b0, ca, cb) + _dg(a1, b1, ca, cb)
    mid = _dg(a0, b1, ca, cb) + _dg(a1, b0, ca, cb)
    return _dg(a0, b0, ca, cb) + (mid + small)


def _mm_exact_lhs(lhs01, x):
    x0, x1, x2 = _split3(x)
    return _dg(lhs01, x0, 1, 0) + (_dg(lhs01, x1, 1, 0) + _dg(lhs01, x2, 1, 0))


def _sigmoid(x):
    return 1.0 / (1.0 + jnp.exp(-x))


def _silu(x):
    return x * _sigmoid(x)


def _softplus(x):
    return jnp.maximum(x, 0.0) + jnp.log(1.0 + jnp.exp(-jnp.abs(x)))


def _rms_rows(x, gain):
    ms = jnp.mean(x * x, axis=-1, keepdims=True)
    return x * lax.rsqrt(ms + NORM_EPS) * gain


def _norm_kernel(x_ref, g_ref, o_ref):
    o_ref[...] = _rms_rows(x_ref[...], g_ref[...]).astype(o_ref.dtype)


def _row_tile(m, want):
    return want if m % want == 0 else m


def rms_norm(x, gain, *, out_dtype=F32, tm=512):
    m, k = x.shape
    tm = _row_tile(m, tm)
    return pl.pallas_call(
        _norm_kernel,
        grid=(m // tm,),
        in_specs=[pl.BlockSpec((tm, k), lambda i: (i, 0)),
                  pl.BlockSpec((1, k), lambda i: (0, 0))],
        out_specs=pl.BlockSpec((tm, k), lambda i: (i, 0)),
        out_shape=jax.ShapeDtypeStruct((m, k), out_dtype),
        compiler_params=_cparams(("parallel",)),
        name="rms_norm",
    )(x, gain)


def _stage_weight(w_ref, wb_ref):
    @pl.when(pl.program_id(1) == 0)
    def _():
        wb_ref[...] = w_ref[...].reshape(wb_ref.shape).astype(BF16)


def _proj_kernel(x_ref, w_ref, o_ref, wb_ref):
    _stage_weight(w_ref, wb_ref)
    o_ref[...] = jnp.dot(x_ref[...], wb_ref[...], preferred_element_type=F32).astype(o_ref.dtype)


def _proj_res_kernel(x_ref, w_ref, r_ref, o_ref, wb_ref, *, scale):
    _stage_weight(w_ref, wb_ref)
    o_ref[...] = r_ref[...] + scale * jnp.dot(x_ref[...], wb_ref[...], preferred_element_type=F32)


def _swiglu_kernel(x_ref, wg_ref, wu_ref, o_ref, wgb_ref, wub_ref):
    _stage_weight(wg_ref, wgb_ref)
    _stage_weight(wu_ref, wub_ref)
    x = x_ref[...]
    gate = jnp.dot(x, wgb_ref[...], preferred_element_type=F32)
    up = jnp.dot(x, wub_ref[...], preferred_element_type=F32)
    o_ref[...] = (_silu(gate) * up).astype(o_ref.dtype)


def _layer_cols(l, k, tn):
    return pl.BlockSpec((None, k, tn), lambda j, i: (l, 0, j))


def project(x, w, l, *, tn=512, tm=1024, out_dtype=F32):
    m, k = x.shape
    n = w.shape[2]
    tm = _row_tile(m, tm)
    return pl.pallas_call(
        _proj_kernel,
        grid=(n // tn, m // tm),
        in_specs=[pl.BlockSpec((tm, k), lambda j, i: (i, 0)), _layer_cols(l, k, tn)],
        out_specs=pl.BlockSpec((tm, tn), lambda j, i: (i, j)),
        out_shape=jax.ShapeDtypeStruct((m, n), out_dtype),
        scratch_shapes=[pltpu.VMEM((k, tn), BF16)],
        compiler_params=_cparams(("parallel", "arbitrary")),
        name="project",
    )(x, w)


def project_residual(x, w, l, res, *, scale, tn=512, tm=512):
    m, k = x.shape
    n = w.shape[2]
    tm = _row_tile(m, tm)
    tile = pl.BlockSpec((tm, tn), lambda j, i: (i, j))
    return pl.pallas_call(
        functools.partial(_proj_res_kernel, scale=scale),
        grid=(n // tn, m // tm),
        in_specs=[pl.BlockSpec((tm, k), lambda j, i: (i, 0)), _layer_cols(l, k, tn), tile],
        out_specs=tile,
        out_shape=jax.ShapeDtypeStruct((m, n), F32),
        scratch_shapes=[pltpu.VMEM((k, tn), BF16)],
        compiler_params=_cparams(("parallel", "arbitrary")),
        name="project_residual",
    )(x, w, res)


def swiglu_in(x, w_in, l, *, tn=512, tm=1024):
    m, k = x.shape
    f = w_in.shape[2] // 2
    tm = _row_tile(m, tm)
    n_tiles = -(-f // tn)

    def start(j):
        return pl.multiple_of(jnp.minimum(j * tn, f - tn), LANE)

    def w_spec(base):
        return pl.BlockSpec((pl.Element(1), pl.Element(k), pl.Element(tn)),
                            lambda j, i: (l, 0, pl.multiple_of(base + start(j), LANE)))

    return pl.pallas_call(
        _swiglu_kernel,
        grid=(n_tiles, m // tm),
        in_specs=[pl.BlockSpec((tm, k), lambda j, i: (i, 0)), w_spec(0), w_spec(f)],
        out_specs=pl.BlockSpec((pl.Element(tm), pl.Element(tn)),
                               lambda j, i: (pl.multiple_of(i * tm, tm), start(j))),
        out_shape=jax.ShapeDtypeStruct((m, f), BF16),
        scratch_shapes=[pltpu.VMEM((k, tn), BF16), pltpu.VMEM((k, tn), BF16)],
        compiler_params=_cparams(("arbitrary", "arbitrary")),
        name="swiglu_in",
    )(x, w_in, w_in)


def _merge_kernel(arw_ref, amo_ref, ass_ref, wrw_ref, wmo_ref, wss_ref,
                  g0_ref, g1_ref, g2_ref, b0_ref, b1_ref, b2_ref, o_ref, wrwb_ref, wmob_ref, wssb_ref):
    _stage_weight(wrw_ref, wrwb_ref)
    _stage_weight(wmo_ref, wmob_ref)
    _stage_weight(wss_ref, wssb_ref)
    y_rw = jnp.dot(arw_ref[...], wrwb_ref[...], preferred_element_type=F32)
    y_mo = jnp.dot(amo_ref[...], wmob_ref[...], preferred_element_type=F32)
    y_ss = jnp.dot(ass_ref[...], wssb_ref[...], preferred_element_type=F32)
    merged = (_sigmoid(g0_ref[...] + b0_ref[...]) * y_rw
              + _sigmoid(g1_ref[...] + b1_ref[...]) * y_mo
              + _sigmoid(g2_ref[...] + b2_ref[...]) * y_ss)
    o_ref[...] = merged.astype(o_ref.dtype)


def gated_merge(a_rw, a_moba, a_ssm, w_rw, w_moba, w_ssm, l, z_gate, b_gate, *, tm=512, tn=512):
    m, k = a_rw.shape
    n = w_rw.shape[2]
    tm = _row_tile(m, tm)
    nj = n // tn
    a_spec = pl.BlockSpec((tm, k), lambda j, i: (i, 0))
    w_spec = _layer_cols(l, k, tn)

    def gate_spec(br):
        return pl.BlockSpec((tm, tn), lambda j, i: (i, br * nj + j))

    def bias_spec(br):
        return pl.BlockSpec((1, tn), lambda j, i: (0, br * nj + j))

    return pl.pallas_call(
        _merge_kernel,
        grid=(nj, m // tm),
        in_specs=[a_spec, a_spec, a_spec, w_spec, w_spec, w_spec,
                  gate_spec(0), gate_spec(1), gate_spec(2),
                  bias_spec(0), bias_spec(1), bias_spec(2)],
        out_specs=pl.BlockSpec((tm, tn), lambda j, i: (i, j)),
        out_shape=jax.ShapeDtypeStruct((m, n), BF16),
        scratch_shapes=[pltpu.VMEM((k, tn), BF16)] * 3,
        compiler_params=_cparams(("parallel", "arbitrary")),
        name="gated_merge",
    )(a_rw, a_moba, a_ssm, w_rw, w_moba, w_ssm, z_gate, z_gate, z_gate, b_gate, b_gate, b_gate)


def _unit_lower_inverse_many(lows, n):
    row = lax.broadcasted_iota(jnp.int32, (n, n), 0)
    col = lax.broadcasted_iota(jnp.int32, (n, n), 1)
    eye = (row == col).astype(F32)
    size = RWKV_INV_BASE

    def same_block(width):
        sh = width.bit_length() - 1
        return jnp.right_shift(row, sh) == jnp.right_shift(col, sh)

    same = same_block(size)
    power = [jnp.where(same, low, 0.0) for low in lows]
    inv = [eye + p for p in power]
    span = 2
    while span < size:
        power = [_mm3(p, p, 1, 0) for p in power]
        inv = [t + _mm3(t, p, 1, 0) for t, p in zip(inv, power)]
        span *= 2
    while size < n:
        same_next = same_block(2 * size)
        pick = same_next & jnp.logical_not(same)
        tmp = [_mm3(t, jnp.where(pick, low, 0.0), 1, 0) for t, low in zip(inv, lows)]
        inv = [t + _mm3(x, t, 1, 0) for t, x in zip(inv, tmp)]
        same = same_next
        size *= 2
    return inv


def _rwkv_kernel(z_ref, sh0_ref, s0_ref, mu_ref, w0_ref, a0_ref, kk_ref, ka_ref, rk_ref,
                 lng_ref, lnb_ref, wup_ref, aup_ref, gup_ref, bd_ref,
                 o_ref, sout_ref, state_ref, prev_ref, *, t_valid):
    C = RWKV_CHUNK
    W = RWKV_WIDTH
    c = pl.program_id(1)

    @pl.when(c == 0)
    def _():
        state_ref[...] = s0_ref[...]
        prev_ref[...] = sh0_ref[...]

    z = z_ref[...]
    row = lax.broadcasted_iota(jnp.int32, (C, 1), 0)
    valid = (c * C + row) < t_valid
    z_prev = jnp.where(row == 0, prev_ref[...], pltpu.roll(z, 1, axis=0))
    prev_ref[...] = z[C - 1:C, :]
    zm = z + (z_prev - z) * mu_ref[...]
    zm = jnp.where(valid, zm, 0.0)
    r = zm[:, 0:W]
    k = zm[:, W:2 * W]
    v = zm[:, 2 * W:3 * W]
    w_lo = zm[:, 3 * W:3 * W + LORA_PAD]
    a_lo = zm[:, 3 * W + LORA_PAD:3 * W + 2 * LORA_PAD]
    g_lo = zm[:, 3 * W + 2 * LORA_PAD:]

    w = -_softplus(-(w0_ref[...] + _mm(jnp.tanh(w_lo), wup_ref[...]))) - 0.5
    logd = jnp.where(valid, -jnp.exp(w), 0.0)
    a = _sigmoid(a0_ref[...] + _mm(a_lo, aup_ref[...]))
    g = _mm(_sigmoid(g_lo), gup_ref[...])

    bd = bd_ref[...]
    n_grp = W // (2 * LANE)

    def head_sum(x):
        parts = _split2(x)
        rows = jnp.concatenate([p[:, g_ * 2 * LANE:(g_ + 1) * 2 * LANE] for p in parts for g_ in range(n_grp)],
                               axis=0)
        res = _dg(rows, bd, 1, 0)
        tot = res[:n_grp * C] + res[n_grp * C:]
        return jnp.concatenate([tot[g_ * C:(g_ + 1) * C] for g_ in range(n_grp)], axis=1)

    kkr = k * kk_ref[...]
    kk = kkr * lax.rsqrt(jnp.maximum(head_sum(kkr * kkr), 1e-12))
    k2 = k * (1.0 + (a - 1.0) * ka_ref[...])
    b = kk * a
    bonus = head_sum(r * k2 * rk_ref[...]) * v

    cum = logd
    shift = 1
    while shift < C:
        cum = cum + jnp.where(row >= shift, pltpu.roll(cum, shift, axis=0), 0.0)
        shift *= 2
    cum_end = cum[C - 1:C, :]
    e_neg = jnp.exp(-cum)
    e_end = jnp.exp(cum_end - cum)
    a_t = -kk * jnp.exp(cum - logd)
    r_t = r * jnp.exp(cum)
    b_t = b * e_neg
    k_t = k2 * e_neg
    b_h = b * e_end
    k_h = k2 * e_end
    p_end = jnp.exp(cum_end)

    rr = lax.broadcasted_iota(jnp.int32, (C, C), 0)
    cc = lax.broadcasted_iota(jnp.int32, (C, C), 1)
    strict = rr > cc
    lower = rr >= cc

    H = range(RWKV_HEADS)
    hs = [slice(h * RWKV_HEAD, (h + 1) * RWKV_HEAD) for h in H]
    s0 = [state_ref[h] for h in H]
    ar = [jnp.concatenate([a_t[:, sl], r_t[:, sl]], axis=0) for sl in hs]
    bk = [jnp.concatenate([b_t[:, sl], k_t[:, sl]], axis=0) for sl in hs]
    vh = [v[:, sl] for sl in hs]
    gram = [_mm(ar[h], bk[h], 1, 1) for h in H]
    ars0 = [_mm(ar[h], s0[h], 1, 1) for h in H]
    a_ab = [jnp.where(strict, g_[:C, :C], 0.0) for g_ in gram]
    a_ak = [jnp.where(strict, g_[:C, C:], 0.0) for g_ in gram]
    r_abk = [jnp.concatenate([jnp.where(lower, g_[C:, :C], 0.0), jnp.where(lower, g_[C:, C:], 0.0)], axis=1)
             for g_ in gram]
    rhs = [ars0[h][:C] + _mm(a_ak[h], vh[h], 1, 0) for h in H]
    inv = _unit_lower_inverse_many(a_ab, C)
    u = [_mm(inv[h], rhs[h], 1, 0) for h in H]
    uv = [jnp.concatenate([u[h], vh[h]], axis=0) for h in H]
    ys = [ars0[h][C:] + _mm(r_abk[h], uv[h], 1, 0) for h in H]
    bkh = [jnp.concatenate([b_h[:, sl], k_h[:, sl]], axis=0) for sl in hs]
    states = [s0[h] * p_end[:, hs[h]] + _mm(uv[h], bkh[h], 0, 0) for h in H]
    state_ref[...] = jnp.stack(states, axis=0)

    y = jnp.concatenate(ys, axis=1)
    inv_n = 1.0 / RWKV_HEAD
    dev = y - head_sum(y) * inv_n
    var = head_sum(dev * dev) * inv_n
    yn = dev * lax.rsqrt(var + LN_X_EPS) * lng_ref[...] + lnb_ref[...]
    o_ref[...] = ((yn + bonus) * g).astype(o_ref.dtype)

    @pl.when(c == pl.num_programs(1) - 1)
    def _():
        sout_ref[...] = state_ref[...]


def rwkv_mix(z, shift0, s0, prm, *, t_valid):
    bsz, t, cols = z.shape
    C = RWKV_CHUNK
    W = RWKV_WIDTH

    def vec(n):
        return pl.BlockSpec((1, n), lambda b, c: (0, 0))

    def mat(r, n):
        return pl.BlockSpec((r, n), lambda b, c: (0, 0))

    return pl.pallas_call(
        functools.partial(_rwkv_kernel, t_valid=t_valid),
        grid=(bsz, t // C),
        in_specs=[pl.BlockSpec((None, C, cols), lambda b, c: (b, c, 0)),
                  pl.BlockSpec((None, 1, cols), lambda b, c: (b, 0, 0)),
                  pl.BlockSpec((None, RWKV_HEADS, RWKV_HEAD, RWKV_HEAD), lambda b, c: (b, 0, 0, 0)),
                  vec(cols), vec(W), vec(W), vec(W), vec(W), vec(W), vec(W), vec(W),
                  mat(LORA_PAD, W), mat(LORA_PAD, W), mat(GATE_LORA_PAD, W),
                  mat(2 * LANE, 2 * LANE)],
        out_specs=[pl.BlockSpec((None, C, W), lambda b, c: (b, c, 0)),
                   pl.BlockSpec((None, RWKV_HEADS, RWKV_HEAD, RWKV_HEAD), lambda b, c: (b, 0, 0, 0))],
        out_shape=[jax.ShapeDtypeStruct((bsz, t, W), BF16),
                   jax.ShapeDtypeStruct((bsz, RWKV_HEADS, RWKV_HEAD, RWKV_HEAD), F32)],
        scratch_shapes=[pltpu.VMEM((RWKV_HEADS, RWKV_HEAD, RWKV_HEAD), F32),
                        pltpu.VMEM((1, cols), F32)],
        compiler_params=_cparams(("parallel", "arbitrary")),
        name="rwkv_mix",
    )(z, shift0, s0, prm['mu'], prm['w0'], prm['a0'], prm['k_k'], prm['k_a'], prm['r_k'],
      prm['ln_g'], prm['ln_b'], prm['w_up'], prm['a_up'], prm['g_up'],
      prm['bd'])


def _ssd_kernel(z_ref, conv0_ref, h0_ref, cw_ref, cb_ref, dtb_ref, alog_ref, dsk_ref, nrm_ref,
                tri_ref, o_ref, hout_ref, state_ref, tail_ref, *, t_valid):
    Q = SSD_CHUNK
    W = SSM_WIDTH
    c = pl.program_id(1)

    @pl.when(c == 0)
    def _():
        state_ref[...] = h0_ref[...]
        tail_ref[...] = conv0_ref[...]

    zall = z_ref[...]
    zg = zall[:, :W]
    x = zall[:, W:W + SSM_CONV_DIM]
    dt_raw = zall[:, W + SSM_CONV_DIM:]
    tail = tail_ref[...]
    tail_ref[...] = x[Q - CONV_TAIL:, :]

    row = lax.broadcasted_iota(jnp.int32, (Q, 1), 0)
    row_t = lax.broadcasted_iota(jnp.int32, (CONV_TAIL, 1), 0)
    cw = cw_ref[...]
    acc = x * cw[CONV_WIDTH - 1:CONV_WIDTH, :]
    for s in range(1, CONV_WIDTH):
        x_s = pltpu.roll(x, s, axis=0)
        t_s = pltpu.roll(tail, s, axis=0)
        top = jnp.where(row_t < s, t_s, x_s[:CONV_TAIL])
        x_s = jnp.concatenate([top, x_s[CONV_TAIL:]], axis=0)
        acc = acc + x_s * cw[CONV_WIDTH - 1 - s:CONV_WIDTH - s, :]
    xbc = _silu(acc + cb_ref[...])
    xs = xbc[:, :W]
    bm = xbc[:, W:W + SSM_GROUPS * SSM_STATE]
    cm = xbc[:, W + SSM_GROUPS * SSM_STATE:]

    valid = (c * Q + row) < t_valid
    dt = jnp.where(valid, _softplus(dt_raw + dtb_ref[...]), 0.0)
    la = dt * (-jnp.exp(alog_ref[...]))
    cs = _mm_exact_lhs(tri_ref[...], la)
    cs_t = cs.T
    cs_end = cs[Q - 1:Q, :]
    to_end = jnp.exp(cs_end - cs)
    from_start = jnp.exp(cs)
    end_decay = jnp.exp(cs_end)

    rr = lax.broadcasted_iota(jnp.int32, (Q, Q), 0)
    cc = lax.broadcasted_iota(jnp.int32, (Q, Q), 1)
    causal = rr >= cc
    heads_per_group = SSM_HEADS // SSM_GROUPS
    H = range(SSM_HEADS)
    grp = [h // heads_per_group for h in H]
    hsl = [slice(h * SSM_HEAD, (h + 1) * SSM_HEAD) for h in H]
    b_g = [bm[:, g * SSM_STATE:(g + 1) * SSM_STATE] for g in range(SSM_GROUPS)]
    c_g = [cm[:, g * SSM_STATE:(g + 1) * SSM_STATE] for g in range(SSM_GROUPS)]
    cb = [_mm(c_g[g], b_g[g], 1, 1) for g in range(SSM_GROUPS)]
    hs = [state_ref[h] for h in H]
    xh = [xs[:, sl] for sl in hsl]
    xdt = [xh[h] * dt[:, h:h + 1] for h in H]
    decay = [jnp.exp(jnp.where(causal, cs[:, h:h + 1] - cs_t[h:h + 1, :], NEG_INF)) for h in H]
    y_diag = [_mm(cb[grp[h]] * decay[h], xdt[h]) for h in H]
    y_off = [_mm(c_g[grp[h]], hs[h], 1, 1) for h in H]
    upd = [_mm(xdt[h] * to_end[:, h:h + 1], b_g[grp[h]], 0, 0) for h in H]
    ys = [y_diag[h] + y_off[h] * from_start[:, h:h + 1] + dsk_ref[:, hsl[h]] * xh[h] for h in H]
    state_ref[...] = jnp.stack([hs[h] * end_decay[:, h:h + 1] + upd[h] for h in H], axis=0)

    yg = jnp.concatenate(ys, axis=1) * _silu(zg)
    group = W // SSM_GROUPS
    parts = []
    for g in range(SSM_GROUPS):
        t = yg[:, g * group:(g + 1) * group]
        parts.append(t * lax.rsqrt(jnp.mean(t * t, axis=-1, keepdims=True) + NORM_EPS))
    o_ref[...] = (jnp.concatenate(parts, axis=1) * nrm_ref[...]).astype(o_ref.dtype)

    @pl.when(c == pl.num_programs(1) - 1)
    def _():
        hout_ref[...] = state_ref[...]


def ssd_mix(z, conv0, h0, prm, *, t_valid):
    bsz, t, cols = z.shape
    Q = SSD_CHUNK
    W = SSM_WIDTH

    def vec(n):
        return pl.BlockSpec((1, n), lambda b, c: (0, 0))

    state_spec = pl.BlockSpec((None, SSM_HEADS, SSM_HEAD, SSM_STATE), lambda b, c: (b, 0, 0, 0))
    return pl.pallas_call(
        functools.partial(_ssd_kernel, t_valid=t_valid),
        grid=(bsz, t // Q),
        in_specs=[pl.BlockSpec((None, Q, cols), lambda b, c: (b, c, 0)),
                  pl.BlockSpec((None, CONV_TAIL, SSM_CONV_DIM), lambda b, c: (b, 0, 0)),
                  state_spec,
                  pl.BlockSpec((CONV_WIDTH, SSM_CONV_DIM), lambda b, c: (0, 0)),
                  vec(SSM_CONV_DIM), vec(SSM_DT_PAD), vec(SSM_DT_PAD), vec(W), vec(W),
                  pl.BlockSpec((Q, Q), lambda b, c: (0, 0))],
        out_specs=[pl.BlockSpec((None, Q, W), lambda b, c: (b, c, 0)), state_spec],
        out_shape=[jax.ShapeDtypeStruct((bsz, t, W), BF16),
                   jax.ShapeDtypeStruct((bsz, SSM_HEADS, SSM_HEAD, SSM_STATE), F32)],
        scratch_shapes=[pltpu.VMEM((SSM_HEADS, SSM_HEAD, SSM_STATE), F32),
                        pltpu.VMEM((CONV_TAIL, SSM_CONV_DIM), F32)],
        compiler_params=_cparams(("parallel", "arbitrary")),
        name="ssd_mix",
    )(z, conv0, h0, prm['conv_w'], prm['conv_b'], prm['dt_bias'], prm['a_log'], prm['d_skip'],
      prm['ssm_norm'], prm['tri'])


def _topk_block_mask(scores, n_allowed):
    col = lax.broadcasted_iota(jnp.int32, scores.shape, 1).astype(F32)
    sel = jnp.zeros(scores.shape, F32)
    for r in range(MOBA_TOPK):
        top = jnp.max(scores, axis=-1, keepdims=True)
        first = jnp.min(jnp.where(scores == top, col, float(scores.shape[1])), axis=-1, keepdims=True)
        pick = col == first
        sel = jnp.where(pick & (n_allowed > r), 1.0, sel)
        scores = jnp.where(pick, REMOVED, scores)
    return sel


def _attend(q, blocks, carry, scale):
    m, l, acc = carry
    qb = q.astype(BF16)
    ss = [jnp.where(keep, _mm(qb, kb, 1, 1) * scale, NEG_INF) for kb, _, keep in blocks]
    m_new = m
    for s in ss:
        m_new = jnp.maximum(m_new, jnp.max(s, axis=-1, keepdims=True))
    alpha = jnp.exp(m - m_new)
    ps = [jnp.where(keep, jnp.exp(s - m_new), 0.0) for s, (_, _, keep) in zip(ss, blocks)]
    l = l * alpha
    acc = acc * alpha
    for p, (_, vb, _) in zip(ps, blocks):
        l = l + jnp.sum(p, axis=-1, keepdims=True)
        acc = acc + _mm(p, vb)
    return m_new, l, acc


def _block_column(sel, n):
    col = lax.broadcasted_iota(jnp.int32, sel.shape, 1)
    return jnp.sum(jnp.where(col == n, sel, 0.0), axis=-1, keepdims=True) > 0.0


def _moba_prompt_kernel(q_ref, k_ref, v_ref, o_ref, kmean_ref, *, n_blocks):
    BLK = MOBA_BLOCK
    i = pl.program_id(2)
    scale = MOBA_HEAD ** -0.5

    @pl.when(i == 0)
    def _():
        kmean_ref[...] = jnp.zeros_like(kmean_ref)
        for n in range(n_blocks):
            kmean_ref[n:n + 1, :] = jnp.mean(k_ref[n * BLK:(n + 1) * BLK, :], axis=0, keepdims=True)

    q = q_ref[...]
    s_blk = _mm6(q, kmean_ref[...], 1, 1)
    col = lax.broadcasted_iota(jnp.int32, s_blk.shape, 1)
    sel = _topk_block_mask(jnp.where(col < i, s_blk, NEG_INF), i)

    def block(n, keep):
        off = pl.multiple_of(n * BLK, BLK)
        return k_ref[pl.ds(off, BLK), :], v_ref[pl.ds(off, BLK), :], keep

    rr = lax.broadcasted_iota(jnp.int32, (BLK, BLK), 0)
    cc = lax.broadcasted_iota(jnp.int32, (BLK, BLK), 1)
    init = (jnp.full((BLK, 1), NEG_INF, F32), jnp.zeros((BLK, 1), F32), jnp.zeros((BLK, MOBA_HEAD), F32))
    carry = _attend(q, [block(i, rr >= cc)], init, scale)

    def past_group(g, carry):
        base = g * MOBA_GROUP
        return _attend(q, [block(base + j, _block_column(sel, base + j)) for j in range(MOBA_GROUP)],
                       carry, scale)

    n_groups = (i + MOBA_GROUP - 1) // MOBA_GROUP
    _, l, acc = lax.fori_loop(0, n_groups, past_group, carry)
    o_ref[...] = (acc / l).astype(o_ref.dtype)


def moba_prompt(qkv):
    bsz, t, _ = qkv.shape
    n_blocks = t // MOBA_BLOCK
    assert t % MOBA_BLOCK == 0 and n_blocks % MOBA_GROUP == 0, "whole groups of key blocks only"
    q_spec = pl.BlockSpec((None, MOBA_BLOCK, MOBA_HEAD), lambda b, h, i: (b, i, h))
    k_spec = pl.BlockSpec((None, t, MOBA_HEAD), lambda b, h, i: (b, 0, MOBA_HEADS + h))
    v_spec = pl.BlockSpec((None, t, MOBA_HEAD), lambda b, h, i: (b, 0, 2 * MOBA_HEADS + h))
    return pl.pallas_call(
        functools.partial(_moba_prompt_kernel, n_blocks=n_blocks),
        grid=(bsz, MOBA_HEADS, n_blocks),
        in_specs=[q_spec, k_spec, v_spec],
        out_specs=q_spec,
        out_shape=jax.ShapeDtypeStruct((bsz, t, MOBA_WIDTH), BF16),
        scratch_shapes=[pltpu.VMEM((LANE, MOBA_HEAD), F32)],
        compiler_params=_cparams(("parallel", "parallel", "arbitrary")),
        name="moba_prompt",
    )(qkv, qkv, qkv)


SAMPLE_ROWS = 8
QROWS = MOBA_HEADS * SAMPLE_ROWS
PAGES_PER_STEP = 8


def _moba_sample_kernel(pt_ref, qbd_ref, knew_ref, vnew_ref, *rest, n_new, pages_per_block):
    G = PAGES_PER_STEP
    k_refs, v_refs = rest[:G], rest[G:2 * G]
    o_ref, ksum_ref, p_ref, l_ref, acc_ref = rest[2 * G:]
    ph = pl.program_id(1)
    g = pl.program_id(2)
    n_steps = pl.num_programs(2)
    n_blocks = (n_steps * G) // pages_per_block
    blocks_per_step = G // pages_per_block
    scale = MOBA_HEAD ** -0.5

    @pl.when(ph == 0)
    def _():
        @pl.when(g == 0)
        def _():
            ksum_ref[...] = jnp.zeros_like(ksum_ref)

        qb = qbd_ref[...].astype(BF16)
        for jb in range(blocks_per_step):
            tot = None
            for jp in range(pages_per_block):
                j = jb * pages_per_block + jp
                kp = k_refs[j][...]
                p_ref[g * G + j] = _mm(qb, kp, 1, 1) * scale
                part = jnp.sum(kp, axis=0, keepdims=True)
                tot = part if tot is None else tot + part
            ksum_ref[pl.ds(g * blocks_per_step + jb, 1), :] = tot

    @pl.when((ph == 1) & (g == 0))
    def _():
        q = qbd_ref[...]
        kmean = ksum_ref[...] * (1.0 / (pages_per_block * PAGE_SIZE))
        s_blk = _mm6(q, kmean, 1, 1)
        col = lax.broadcasted_iota(jnp.int32, s_blk.shape, 1)
        sel = _topk_block_mask(jnp.where(col < n_blocks, s_blk, NEG_INF), n_blocks)
        rr = jnp.bitwise_and(lax.broadcasted_iota(jnp.int32, (QROWS, PAGE_SIZE), 0), SAMPLE_ROWS - 1)
        cc = lax.broadcasted_iota(jnp.int32, (QROWS, PAGE_SIZE), 1)
        own_keep = (cc <= rr) & (cc < n_new)
        s_own = jnp.where(own_keep, _mm(q, knew_ref[...], 1, 1) * scale, NEG_INF)

        def block_max(b, m):
            keep = _block_column(sel, b)
            for jp in range(pages_per_block):
                s = jnp.where(keep, p_ref[b * pages_per_block + jp], NEG_INF)
                m = jnp.maximum(m, jnp.max(s, axis=-1, keepdims=True))
            return m

        m = lax.fori_loop(0, n_blocks, block_max, jnp.max(s_own, axis=-1, keepdims=True))

        def block_prob(b, l):
            keep = _block_column(sel, b)
            for jp in range(pages_per_block):
                pg = b * pages_per_block + jp
                p = jnp.where(keep, jnp.exp(p_ref[pg] - m), 0.0)
                p_ref[pg] = p
                l = l + jnp.sum(p, axis=-1, keepdims=True)
            return l

        p_own = jnp.where(own_keep, jnp.exp(s_own - m), 0.0)
        l_ref[...] = lax.fori_loop(0, n_blocks, block_prob, jnp.sum(p_own, axis=-1, keepdims=True))
        acc_ref[...] = _mm(p_own, vnew_ref[...])

    @pl.when(ph == 1)
    def _():
        tot = None
        for j in range(G):
            part = _mm(p_ref[g * G + j], v_refs[j][...])
            tot = part if tot is None else tot + part
        acc_ref[...] += tot

    @pl.when((ph == 1) & (g == n_steps - 1))
    def _():
        out = acc_ref[...] / l_ref[...]
        for h in range(MOBA_HEADS):
            o_ref[:, h * MOBA_HEAD:(h + 1) * MOBA_HEAD] = out[
                h * SAMPLE_ROWS:(h + 1) * SAMPLE_ROWS, h * MOBA_HEAD:(h + 1) * MOBA_HEAD]


def moba_sample(qbd, k_new, v_new, pool_k, pool_v, page_table, *, n_new):
    bsz, n_pages = page_table.shape
    G = PAGES_PER_STEP
    pages_per_block = MOBA_BLOCK // PAGE_SIZE
    assert n_pages % G == 0 and G % pages_per_block == 0, "whole steps of whole blocks only"
    n_steps = n_pages // G
    row_spec = pl.BlockSpec((None, SAMPLE_ROWS, MOBA_WIDTH), lambda b, ph, g, pt: (b, 0, 0))
    new_spec = pl.BlockSpec((None, PAGE_SIZE, MOBA_WIDTH), lambda b, ph, g, pt: (b, 0, 0))

    def k_spec(j):
        return pl.BlockSpec((None, PAGE_SIZE, MOBA_WIDTH),
                            lambda b, ph, g, pt: (pt[b, (g * (1 - ph) + (n_steps - 1) * ph) * G + j], 0, 0))

    def v_spec(j):
        return pl.BlockSpec((None, PAGE_SIZE, MOBA_WIDTH),
                            lambda b, ph, g, pt: (pt[b, g * ph * G + j], 0, 0))

    grid_spec = pltpu.PrefetchScalarGridSpec(
        num_scalar_prefetch=1,
        grid=(bsz, 2, n_steps),
        in_specs=[pl.BlockSpec((None, QROWS, MOBA_WIDTH), lambda b, ph, g, pt: (b, 0, 0)),
                  new_spec, new_spec]
                 + [k_spec(j) for j in range(G)] + [v_spec(j) for j in range(G)],
        out_specs=row_spec,
        scratch_shapes=[pltpu.VMEM((LANE, MOBA_WIDTH), F32),
                        pltpu.VMEM((n_pages, QROWS, PAGE_SIZE), F32),
                        pltpu.VMEM((QROWS, 1), F32),
                        pltpu.VMEM((QROWS, MOBA_WIDTH), F32)])
    return pl.pallas_call(
        functools.partial(_moba_sample_kernel, n_new=n_new, pages_per_block=pages_per_block),
        grid_spec=grid_spec,
        out_shape=jax.ShapeDtypeStruct((bsz, SAMPLE_ROWS, MOBA_WIDTH), F32),
        compiler_params=_cparams(("parallel", "arbitrary", "arbitrary")),
        name="moba_sample",
    )(page_table, qbd, k_new, v_new, *([pool_k] * G), *([pool_v] * G))


def _cross_kernel(q_ref, k_ref, v_ref, o_ref):
    scale = MEM_HEAD ** -0.5
    for h in range(MEM_HEADS):
        sl = slice(h * MEM_HEAD, (h + 1) * MEM_HEAD)
        s = _mm(q_ref[:, sl], k_ref[:, sl], 1, 1) * scale
        e = jnp.exp(s - jnp.max(s, axis=-1, keepdims=True))
        pr = e / jnp.sum(e, axis=-1, keepdims=True)
        o_ref[:, sl] = _mm(pr, v_ref[:, sl]).astype(o_ref.dtype)


def cross_attention(q, mk, mv, *, tq=512):
    bsz, t, d = q.shape
    n_mem = mk.shape[1]
    tq = _row_tile(t, tq)
    kv_spec = pl.BlockSpec((None, n_mem, d), lambda b, i: (b, 0, 0))
    q_spec = pl.BlockSpec((None, tq, d), lambda b, i: (b, i, 0))
    return pl.pallas_call(
        _cross_kernel,
        grid=(bsz, t // tq),
        in_specs=[q_spec, kv_spec, kv_spec],
        out_specs=q_spec,
        out_shape=jax.ShapeDtypeStruct((bsz, t, d), BF16),
        compiler_params=_cparams(("parallel", "parallel")),
        name="cross_attention",
    )(q, mk, mv)


def _pad_cols(x, n):
    return jnp.pad(x, [(0, 0)] * (x.ndim - 1) + [(0, n - x.shape[-1])])


def _pad_rows(x, n):
    return jnp.pad(x, [(0, n - x.shape[0])] + [(0, 0)] * (x.ndim - 1))


def _rwkv_cols_padded(x):
    W = RWKV_WIDTH
    o = 3 * W
    return jnp.concatenate([
        x[..., :o],
        _pad_cols(x[..., o:o + DECAY_LORA], LORA_PAD),
        _pad_cols(x[..., o + DECAY_LORA:o + DECAY_LORA + AAA_LORA], LORA_PAD),
        _pad_cols(x[..., o + DECAY_LORA + AAA_LORA:], GATE_LORA_PAD)], axis=-1)


def _rwkv_cols_unpadded(x):
    o = 3 * RWKV_WIDTH
    return jnp.concatenate([
        x[..., :o],
        x[..., o:o + DECAY_LORA],
        x[..., o + LORA_PAD:o + LORA_PAD + AAA_LORA],
        x[..., o + 2 * LORA_PAD:o + 2 * LORA_PAD + GATE_LORA]], axis=-1)


def _constants():
    tri_ssd = jnp.asarray(np.tril(np.ones((SSD_CHUNK, SSD_CHUNK), np.float32)), BF16)
    lane_head = np.arange(2 * LANE) // RWKV_HEAD
    same_head = (lane_head[:, None] == lane_head[None, :]).astype(np.float32)
    return tri_ssd, jnp.asarray(same_head, BF16)


def _layer_params(l, P, consts):
    tri_ssd, same_head = consts
    row = lambda x: x.reshape(1, -1).astype(F32)
    w_in = P['w_in'][l]
    o_moba = RWKV_COLS
    o_ssm = o_moba + 3 * MOBA_WIDTH
    o_gate = o_ssm + SSM_COLS
    lp = {}
    for name in ('norm_ffn1', 'norm_ffn2', 'norm_mix', 'norm_cross', 'norm_mem', 'b_gate'):
        lp[name] = row(P[name][l])
    lp['w_rw'] = _rwkv_cols_padded(w_in[:, :RWKV_COLS]).astype(BF16)[None]
    lp['w_qkv'] = w_in[:, o_moba:o_ssm].astype(BF16)[None]
    lp['w_ssm'] = _pad_cols(w_in[:, o_ssm:o_gate], SSM_COLS_PAD).astype(BF16)[None]
    lp['w_gate'] = w_in[:, o_gate:].astype(BF16)[None]
    lp['rw'] = {
        'mu': _rwkv_cols_padded(row(P['rw_mu'][l])),
        'w0': row(P['rw_w0'][l]), 'a0': row(P['rw_a0'][l]),
        'k_k': row(P['rw_k_k'][l]), 'k_a': row(P['rw_k_a'][l]), 'r_k': row(P['rw_r_k'][l]),
        'ln_g': row(P['rw_ln_g'][l]), 'ln_b': row(P['rw_ln_b'][l]),
        'w_up': _pad_rows(P['rw_w_up'][l], LORA_PAD).astype(BF16),
        'a_up': _pad_rows(P['rw_a_up'][l], LORA_PAD).astype(BF16),
        'g_up': _pad_rows(P['rw_g_up'][l], GATE_LORA_PAD).astype(BF16),
        'bd': same_head,
    }
    lp['ssm'] = {
        'conv_w': P['conv_w'][l].astype(F32), 'conv_b': row(P['conv_b'][l]),
        'dt_bias': _pad_cols(row(P['dt_bias'][l]), SSM_DT_PAD),
        'a_log': _pad_cols(row(P['a_log'][l]), SSM_DT_PAD),
        'd_skip': row(jnp.repeat(P['d_skip'][l], SSM_HEAD)),
        'ssm_norm': row(P['ssm_norm'][l]),
        'tri': tri_ssd,
    }
    return lp


def _ffn(x, gain, w_in, w_out, l):
    act = swiglu_in(rms_norm(x, gain, out_dtype=BF16), w_in, l)
    return project_residual(act, w_out, l, x, scale=0.5)


def _pad_time(x, t_pad):
    return jnp.pad(x, ((0, 0), (0, t_pad - x.shape[1]), (0, 0)))


def _trunk_layer(x, l, lp, P, bsz, t, rw_shift0, rw_s0, conv0, ssm_h0, moba_fn, mem_k, mem_v):
    m = bsz * t
    h = _ffn(x, lp['norm_ffn1'], P['w_ffn1_in'], P['w_ffn1_out'], l)

    u = rms_norm(h, lp['norm_mix'], out_dtype=BF16)
    z_rw = project(u, lp['w_rw'], 0)
    qkv = project(u, lp['w_qkv'], 0)
    z_ssm = project(u, lp['w_ssm'], 0, tn=640)
    z_gate = project(u, lp['w_gate'], 0)
    k = qkv[:, MOBA_WIDTH:2 * MOBA_WIDTH]
    v = qkv[:, 2 * MOBA_WIDTH:]

    t_rw = -(-t // RWKV_CHUNK) * RWKV_CHUNK
    z_rw3 = z_rw.reshape(bsz, t, RWKV_COLS_PAD)
    a_rw, rw_s = rwkv_mix(_pad_time(z_rw3, t_rw), _rwkv_cols_padded(rw_shift0)[:, None, :], rw_s0,
                          lp['rw'], t_valid=t)
    a_rw = a_rw[:, :t].reshape(m, RWKV_WIDTH)
    rw_shift = _rwkv_cols_unpadded(z_rw3[:, t - 1])

    a_moba = moba_fn(qkv.reshape(bsz, t, 3 * MOBA_WIDTH)).reshape(m, MOBA_WIDTH)

    t_ssd = -(-t // SSD_CHUNK) * SSD_CHUNK
    z_ssm3 = z_ssm.reshape(bsz, t, SSM_COLS_PAD)
    conv0_pad = jnp.pad(conv0, ((0, 0), (CONV_TAIL - (CONV_WIDTH - 1), 0), (0, 0)))
    a_ssm, ssm_h = ssd_mix(_pad_time(z_ssm3, t_ssd), conv0_pad, ssm_h0, lp['ssm'], t_valid=t)
    a_ssm = a_ssm[:, :t].reshape(m, SSM_WIDTH)
    keep = min(t, CONV_WIDTH - 1)
    xbc_tail = z_ssm3[:, t - keep:, SSM_WIDTH:SSM_WIDTH + SSM_CONV_DIM]
    conv_new = jnp.concatenate([conv0, xbc_tail], axis=1)[:, -(CONV_WIDTH - 1):]

    merged = gated_merge(a_rw, a_moba, a_ssm, P['rw_out'], P['moba_out'], P['ssm_out'], l,
                         z_gate, lp['b_gate'])
    h = project_residual(merged, P['w_mix_out'], l, h, scale=1.0)

    cq = project(rms_norm(h, lp['norm_cross'], out_dtype=BF16), P['w_cq'], l)
    t_ca = -(-t // 8) * 8
    o = cross_attention(_pad_time(cq.reshape(bsz, t, D_MODEL), t_ca), mem_k, mem_v)
    o = o[:, :t].reshape(m, D_MODEL)
    h = project_residual(o, P['w_co'], l, h, scale=1.0)

    h = _ffn(h, lp['norm_ffn2'], P['w_ffn2_in'], P['w_ffn2_out'], l)
    k4 = k.reshape(bsz, t, MOBA_HEADS, MOBA_HEAD)
    v4 = v.reshape(bsz, t, MOBA_HEADS, MOBA_HEAD)
    return h, (k4, v4, rw_s, rw_shift, conv_new, ssm_h)


def _moba_sample_fn(qkv, *, pool_k, pool_v, page_table):
    q, k, v = (qkv[..., i * MOBA_WIDTH:(i + 1) * MOBA_WIDTH] for i in range(3))
    bsz, s_new, _ = q.shape
    qp = _pad_time(q, SAMPLE_ROWS).reshape(bsz, SAMPLE_ROWS, MOBA_HEADS, MOBA_HEAD)
    eye = jnp.eye(MOBA_HEADS, dtype=F32)
    qbd = jnp.einsum('bthd,hg->bhtgd', qp, eye).reshape(bsz, QROWS, MOBA_WIDTH)
    out = moba_sample(qbd, _pad_time(k, PAGE_SIZE), _pad_time(v, PAGE_SIZE), pool_k, pool_v,
                      page_table, n_new=s_new)
    return out[:, :s_new].astype(BF16)


def kernel(x_prompt, x_sample, cache_k, cache_v, cache_mem_k, cache_mem_v, state_rwkv, state_rwkv_shift, state_conv, state_ssm, page_table, mem_prompt, norm_ffn1, w_ffn1_in, w_ffn1_out, norm_mix, w_in, rw_mu, rw_w0, rw_w_up, rw_a0, rw_a_up, rw_g_up, rw_k_k, rw_k_a, rw_r_k, rw_ln_g, rw_ln_b, rw_out, moba_out, conv_w, conv_b, dt_bias, a_log, d_skip, ssm_norm, ssm_out, b_gate, w_mix_out, norm_cross, norm_mem, w_cq, w_ckv, w_co, norm_ffn2, w_ffn2_in, w_ffn2_out, norm_final):
    P = dict(norm_ffn1=norm_ffn1, w_ffn1_in=w_ffn1_in, w_ffn1_out=w_ffn1_out, norm_mix=norm_mix,
             w_in=w_in, rw_mu=rw_mu, rw_w0=rw_w0, rw_w_up=rw_w_up, rw_a0=rw_a0, rw_a_up=rw_a_up,
             rw_g_up=rw_g_up, rw_k_k=rw_k_k, rw_k_a=rw_k_a, rw_r_k=rw_r_k, rw_ln_g=rw_ln_g,
             rw_ln_b=rw_ln_b, rw_out=rw_out, moba_out=moba_out, conv_w=conv_w, conv_b=conv_b,
             dt_bias=dt_bias, a_log=a_log, d_skip=d_skip, ssm_norm=ssm_norm, ssm_out=ssm_out,
             b_gate=b_gate, w_mix_out=w_mix_out, norm_cross=norm_cross, norm_mem=norm_mem,
             w_cq=w_cq, w_ckv=w_ckv, w_co=w_co, norm_ffn2=norm_ffn2, w_ffn2_in=w_ffn2_in,
             w_ffn2_out=w_ffn2_out)
    depth = w_in.shape[0]
    bp, tp, d = x_prompt.shape
    bs, ts, _ = x_sample.shape
    n_mem = mem_prompt.shape[1]
    consts = _constants()
    n_pool = cache_k.shape[1]
    pool_k = cache_k.reshape(depth * n_pool, PAGE_SIZE, MOBA_WIDTH)
    pool_v = cache_v.reshape(depth * n_pool, PAGE_SIZE, MOBA_WIDTH)
    h_p = x_prompt.reshape(bp * tp, d)
    h_s = x_sample.reshape(bs * ts, d)
    mem2 = mem_prompt.reshape(bp * n_mem, d)
    outs_p = [[] for _ in range(8)]
    outs_s = [[] for _ in range(6)]
    for l in range(depth):
        lp = _layer_params(l, P, consts)
        mkv = project(rms_norm(mem2, lp['norm_mem'], out_dtype=BF16), w_ckv, l)
        mk_p = mkv[:, :d].reshape(bp, n_mem, d)
        mv_p = mkv[:, d:].reshape(bp, n_mem, d)
        h_p, st = _trunk_layer(
            h_p, l, lp, P, bp, tp,
            jnp.zeros((bp, RWKV_COLS), F32),
            jnp.zeros((bp, RWKV_HEADS, RWKV_HEAD, RWKV_HEAD), F32),
            jnp.zeros((bp, CONV_WIDTH - 1, SSM_CONV_DIM), F32),
            jnp.zeros((bp, SSM_HEADS, SSM_HEAD, SSM_STATE), F32),
            moba_prompt, mk_p, mv_p)
        for dst, val in zip(outs_p, (st[0], st[1], mk_p.reshape(bp, n_mem, MEM_HEADS, MEM_HEAD),
                                     mv_p.reshape(bp, n_mem, MEM_HEADS, MEM_HEAD),
                                     st[2], st[3], st[4], st[5])):
            dst.append(val)
        moba_fn = functools.partial(_moba_sample_fn, pool_k=pool_k, pool_v=pool_v,
                                    page_table=page_table + l * n_pool)
        h_s, st = _trunk_layer(
            h_s, l, lp, P, bs, ts, state_rwkv_shift[l], state_rwkv[l], state_conv[l], state_ssm[l],
            moba_fn, cache_mem_k[l].reshape(bs, n_mem, d), cache_mem_v[l].reshape(bs, n_mem, d))
        for dst, val in zip(outs_s, st):
            dst.append(val)
    gain = norm_final.reshape(1, d)
    y_prompt = rms_norm(h_p, gain).reshape(bp, tp, d)
    y_sample = rms_norm(h_s, gain).reshape(bs, ts, d)
    return (y_prompt, y_sample) + tuple(jnp.stack(o) for o in outs_p) + tuple(jnp.stack(o) for o in outs_s)
```

```python
import functools

import jax
import jax.numpy as jnp
import numpy as np
from jax import lax
from jax.experimental import pallas as pl
from jax.experimental.pallas import tpu as pltpu

F32 = jnp.float32
BF16 = jnp.bfloat16

D_MODEL = 2048
NORM_EPS = 1e-6
NEG_INF = -1e30
REMOVED = -3e38

RWKV_WIDTH = 1024
RWKV_HEAD = 64
RWKV_HEADS = 16
DECAY_LORA = 64
AAA_LORA = 64
GATE_LORA = 160
RWKV_COLS = 3 * RWKV_WIDTH + DECAY_LORA + AAA_LORA + GATE_LORA
LN_X_EPS = 64e-5
LORA_PAD = 128
GATE_LORA_PAD = 256
RWKV_COLS_PAD = 3 * RWKV_WIDTH + 2 * LORA_PAD + GATE_LORA_PAD
RWKV_CHUNK = 64
RWKV_INV_BASE = 8

MOBA_WIDTH = 1024
MOBA_HEAD = 128
MOBA_HEADS = 8
MOBA_BLOCK = 256
MOBA_TOPK = 3
MOBA_GROUP = 4
PAGE_SIZE = 128

SSM_WIDTH = 1024
SSM_HEAD = 64
SSM_HEADS = 16
SSM_GROUPS = 4
SSM_STATE = 128
CONV_WIDTH = 4
SSM_CONV_DIM = SSM_WIDTH + 2 * SSM_GROUPS * SSM_STATE
SSM_COLS = SSM_WIDTH + SSM_CONV_DIM + SSM_HEADS
SSM_DT_PAD = 128
SSM_COLS_PAD = SSM_WIDTH + SSM_CONV_DIM + SSM_DT_PAD
SSD_CHUNK = 128
CONV_TAIL = 8

N_BRANCH = 3
MEM_HEADS = 4
MEM_HEAD = 512

LANE = 128
VMEM_LIMIT = 56 * 1024 * 1024


def _cparams(sem):
    return pltpu.CompilerParams(dimension_semantics=sem, vmem_limit_bytes=VMEM_LIMIT)


def _dg(a, b, ca, cb):
    return lax.dot_general(a, b, (((ca,), (cb,)), ((), ())), preferred_element_type=F32)


def _mm(a, b, ca=1, cb=0):
    return _dg(a.astype(BF16), b.astype(BF16), ca, cb)


def _split2(x):
    hi = x.astype(BF16)
    lo = (x - hi.astype(F32)).astype(BF16)
    return hi, lo


def _split3(x):
    hi = x.astype(BF16)
    r = x - hi.astype(F32)
    mid = r.astype(BF16)
    lo = (r - mid.astype(F32)).astype(BF16)
    return hi, mid, lo


def _mm3(a, b, ca=1, cb=0):
    ah, al = _split2(a)
    bh, bl = _split2(b)
    return _dg(ah, bh, ca, cb) + (_dg(ah, bl, ca, cb) + _dg(al, bh, ca, cb))


def _mm6(a, b, ca=1, cb=0):
    a0, a1, a2 = _split3(a)
    b0, b1, b2 = _split3(b)
    small = _dg(a0, b2, ca, cb) + _dg(a2, b0, ca, cb) + _dg(a1, b1, ca, cb)
    mid = _dg(a0, b1, ca, cb) + _dg(a1, b0, ca, cb)
    return _dg(a0, b0, ca, cb) + (mid + small)


def _mm_exact_lhs(lhs01, x):
    x0, x1, x2 = _split3(x)
    return _dg(lhs01, x0, 1, 0) + (_dg(lhs01, x1, 1, 0) + _dg(lhs01, x2, 1, 0))


def _sigmoid(x):
    return 1.0 / (1.0 + jnp.exp(-x))


def _silu(x):
    return x * _sigmoid(x)


def _softplus(x):
    return jnp.maximum(x, 0.0) + jnp.log(1.0 + jnp.exp(-jnp.abs(x)))


def _rms_rows(x, gain):
    ms = jnp.mean(x * x, axis=-1, keepdims=True)
    return x * lax.rsqrt(ms + NORM_EPS) * gain


def _norm_kernel(x_ref, g_ref, o_ref):
    o_ref[...] = _rms_rows(x_ref[...], g_ref[...]).astype(o_ref.dtype)


def _row_tile(m, want):
    return want if m % want == 0 else m


def rms_norm(x, gain, *, out_dtype=F32, tm=512):
    m, k = x.shape
    tm = _row_tile(m, tm)
    return pl.pallas_call(
        _norm_kernel,
        grid=(m // tm,),
        in_specs=[pl.BlockSpec((tm, k), lambda i: (i, 0)),
                  pl.BlockSpec((1, k), lambda i: (0, 0))],
        out_specs=pl.BlockSpec((tm, k), lambda i: (i, 0)),
        out_shape=jax.ShapeDtypeStruct((m, k), out_dtype),
        compiler_params=_cparams(("parallel",)),
        name="rms_norm",
    )(x, gain)


def _stage_weight(w_ref, wb_ref):
    @pl.when(pl.program_id(1) == 0)
    def _():
        wb_ref[...] = w_ref[...].reshape(wb_ref.shape).astype(BF16)


def _proj_kernel(x_ref, w_ref, o_ref, wb_ref):
    _stage_weight(w_ref, wb_ref)
    o_ref[...] = jnp.dot(x_ref[...], wb_ref[...], preferred_element_type=F32).astype(o_ref.dtype)


def _proj_res_kernel(x_ref, w_ref, r_ref, o_ref, wb_ref, *, scale):
    _stage_weight(w_ref, wb_ref)
    o_ref[...] = r_ref[...] + scale * jnp.dot(x_ref[...], wb_ref[...], preferred_element_type=F32)


def _swiglu_kernel(x_ref, wg_ref, wu_ref, o_ref, wgb_ref, wub_ref):
    _stage_weight(wg_ref, wgb_ref)
    _stage_weight(wu_ref, wub_ref)
    x = x_ref[...]
    gate = jnp.dot(x, wgb_ref[...], preferred_element_type=F32)
    up = jnp.dot(x, wub_ref[...], preferred_element_type=F32)
    o_ref[...] = (_silu(gate) * up).astype(o_ref.dtype)


def _layer_cols(l, k, tn):
    return pl.BlockSpec((None, k, tn), lambda j, i: (l, 0, j))


def project(x, w, l, *, tn=512, tm=1024, out_dtype=F32):
    m, k = x.shape
    n = w.shape[2]
    tm = _row_tile(m, tm)
    return pl.pallas_call(
        _proj_kernel,
        grid=(n // tn, m // tm),
        in_specs=[pl.BlockSpec((tm, k), lambda j, i: (i, 0)), _layer_cols(l, k, tn)],
        out_specs=pl.BlockSpec((tm, tn), lambda j, i: (i, j)),
        out_shape=jax.ShapeDtypeStruct((m, n), out_dtype),
        scratch_shapes=[pltpu.VMEM((k, tn), BF16)],
        compiler_params=_cparams(("parallel", "arbitrary")),
        name="project",
    )(x, w)


def project_residual(x, w, l, res, *, scale, tn=512, tm=512):
    m, k = x.shape
    n = w.shape[2]
    tm = _row_tile(m, tm)
    tile = pl.BlockSpec((tm, tn), lambda j, i: (i, j))
    return pl.pallas_call(
        functools.partial(_proj_res_kernel, scale=scale),
        grid=(n // tn, m // tm),
        in_specs=[pl.BlockSpec((tm, k), lambda j, i: (i, 0)), _layer_cols(l, k, tn), tile],
        out_specs=tile,
        out_shape=jax.ShapeDtypeStruct((m, n), F32),
        scratch_shapes=[pltpu.VMEM((k, tn), BF16)],
        compiler_params=_cparams(("parallel", "arbitrary")),
        name="project_residual",
    )(x, w, res)


def swiglu_in(x, w_in, l, *, tn=512, tm=1024):
    m, k = x.shape
    f = w_in.shape[2] // 2
    tm = _row_tile(m, tm)
    n_tiles = -(-f // tn)

    def start(j):
        return pl.multiple_of(jnp.minimum(j * tn, f - tn), LANE)

    def w_spec(base):
        return pl.BlockSpec((pl.Element(1), pl.Element(k), pl.Element(tn)),
                            lambda j, i: (l, 0, pl.multiple_of(base + start(j), LANE)))

    return pl.pallas_call(
        _swiglu_kernel,
        grid=(n_tiles, m // tm),
        in_specs=[pl.BlockSpec((tm, k), lambda j, i: (i, 0)), w_spec(0), w_spec(f)],
        out_specs=pl.BlockSpec((pl.Element(tm), pl.Element(tn)),
                               lambda j, i: (pl.multiple_of(i * tm, tm), start(j))),
        out_shape=jax.ShapeDtypeStruct((m, f), BF16),
        scratch_shapes=[pltpu.VMEM((k, tn), BF16), pltpu.VMEM((k, tn), BF16)],
        compiler_params=_cparams(("arbitrary", "arbitrary")),
        name="swiglu_in",
    )(x, w_in, w_in)


def _merge_kernel(arw_ref, amo_ref, ass_ref, wrw_ref, wmo_ref, wss_ref,
                  g0_ref, g1_ref, g2_ref, b0_ref, b1_ref, b2_ref, o_ref, wrwb_ref, wmob_ref, wssb_ref):
    _stage_weight(wrw_ref, wrwb_ref)
    _stage_weight(wmo_ref, wmob_ref)
    _stage_weight(wss_ref, wssb_ref)
    y_rw = jnp.dot(arw_ref[...], wrwb_ref[...], preferred_element_type=F32)
    y_mo = jnp.dot(amo_ref[...], wmob_ref[...], preferred_element_type=F32)
    y_ss = jnp.dot(ass_ref[...], wssb_ref[...], preferred_element_type=F32)
    merged = (_sigmoid(g0_ref[...] + b0_ref[...]) * y_rw
              + _sigmoid(g1_ref[...] + b1_ref[...]) * y_mo
              + _sigmoid(g2_ref[...] + b2_ref[...]) * y_ss)
    o_ref[...] = merged.astype(o_ref.dtype)


def gated_merge(a_rw, a_moba, a_ssm, w_rw, w_moba, w_ssm, l, z_gate, b_gate, *, tm=512, tn=512):
    m, k = a_rw.shape
    n = w_rw.shape[2]
    tm = _row_tile(m, tm)
    nj = n // tn
    a_spec = pl.BlockSpec((tm, k), lambda j, i: (i, 0))
    w_spec = _layer_cols(l, k, tn)

    def gate_spec(br):
        return pl.BlockSpec((tm, tn), lambda j, i: (i, br * nj + j))

    def bias_spec(br):
        return pl.BlockSpec((1, tn), lambda j, i: (0, br * nj + j))

    return pl.pallas_call(
        _merge_kernel,
        grid=(nj, m // tm),
        in_specs=[a_spec, a_spec, a_spec, w_spec, w_spec, w_spec,
                  gate_spec(0), gate_spec(1), gate_spec(2),
                  bias_spec(0), bias_spec(1), bias_spec(2)],
        out_specs=pl.BlockSpec((tm, tn), lambda j, i: (i, j)),
        out_shape=jax.ShapeDtypeStruct((m, n), BF16),
        scratch_shapes=[pltpu.VMEM((k, tn), BF16)] * 3,
        compiler_params=_cparams(("parallel", "arbitrary")),
        name="gated_merge",
    )(a_rw, a_moba, a_ssm, w_rw, w_moba, w_ssm, z_gate, z_gate, z_gate, b_gate, b_gate, b_gate)


def _unit_lower_inverse_many(lows, n):
    row = lax.broadcasted_iota(jnp.int32, (n, n), 0)
    col = lax.broadcasted_iota(jnp.int32, (n, n), 1)
    eye = (row == col).astype(F32)
    size = RWKV_INV_BASE

    def same_block(width):
        sh = width.bit_length() - 1
        return jnp.right_shift(row, sh) == jnp.right_shift(col, sh)

    same = same_block(size)
    power = [jnp.where(same, low, 0.0) for low in lows]
    inv = [eye + p for p in power]
    span = 2
    while span < size:
        power = [_mm3(p, p, 1, 0) for p in power]
        inv = [t + _mm3(t, p, 1, 0) for t, p in zip(inv, power)]
        span *= 2
    while size < n:
        same_next = same_block(2 * size)
        pick = same_next & jnp.logical_not(same)
        tmp = [_mm3(t, jnp.where(pick, low, 0.0), 1, 0) for t, low in zip(inv, lows)]
        inv = [t + _mm3(x, t, 1, 0) for t, x in zip(inv, tmp)]
        same = same_next
        size *= 2
    return inv


def _rwkv_kernel(z_ref, sh0_ref, s0_ref, mu_ref, w0_ref, a0_ref, kk_ref, ka_ref, rk_ref,
                 lng_ref, lnb_ref, wup_ref, aup_ref, gup_ref, bd_ref,
                 o_ref, sout_ref, state_ref, prev_ref, *, t_valid):
    C = RWKV_CHUNK
    W = RWKV_WIDTH
    c = pl.program_id(1)

    @pl.when(c == 0)
    def _():
        state_ref[...] = s0_ref[...]
        prev_ref[...] = sh0_ref[...]

    z = z_ref[...]
    row = lax.broadcasted_iota(jnp.int32, (C, 1), 0)
    valid = (c * C + row) < t_valid
    z_prev = jnp.where(row == 0, prev_ref[...], pltpu.roll(z, 1, axis=0))
    prev_ref[...] = z[C - 1:C, :]
    zm = z + (z_prev - z) * mu_ref[...]
    zm = jnp.where(valid, zm, 0.0)
    r = zm[:, 0:W]
    k = zm[:, W:2 * W]
    v = zm[:, 2 * W:3 * W]
    w_lo = zm[:, 3 * W:3 * W + LORA_PAD]
    a_lo = zm[:, 3 * W + LORA_PAD:3 * W + 2 * LORA_PAD]
    g_lo = zm[:, 3 * W + 2 * LORA_PAD:]

    w = -_softplus(-(w0_ref[...] + _mm(jnp.tanh(w_lo), wup_ref[...]))) - 0.5
    logd = jnp.where(valid, -jnp.exp(w), 0.0)
    a = _sigmoid(a0_ref[...] + _mm(a_lo, aup_ref[...]))
    g = _mm(_sigmoid(g_lo), gup_ref[...])

    bd = bd_ref[...]
    n_grp = W // (2 * LANE)

    def head_sum(x):
        parts = _split2(x)
        rows = jnp.concatenate([p[:, g_ * 2 * LANE:(g_ + 1) * 2 * LANE] for p in parts for g_ in range(n_grp)],
                               axis=0)
        res = _dg(rows, bd, 1, 0)
        tot = res[:n_grp * C] + res[n_grp * C:]
        return jnp.concatenate([tot[g_ * C:(g_ + 1) * C] for g_ in range(n_grp)], axis=1)

    kkr = k * kk_ref[...]
    kk = kkr * lax.rsqrt(jnp.maximum(head_sum(kkr * kkr), 1e-12))
    k2 = k * (1.0 + (a - 1.0) * ka_ref[...])
    b = kk * a
    bonus = head_sum(r * k2 * rk_ref[...]) * v

    cum = logd
    shift = 1
    while shift < C:
        cum = cum + jnp.where(row >= shift, pltpu.roll(cum, shift, axis=0), 0.0)
        shift *= 2
    cum_end = cum[C - 1:C, :]
    e_neg = jnp.exp(-cum)
    e_end = jnp.exp(cum_end - cum)
    a_t = -kk * jnp.exp(cum - logd)
    r_t = r * jnp.exp(cum)
    b_t = b * e_neg
    k_t = k2 * e_neg
    b_h = b * e_end
    k_h = k2 * e_end
    p_end = jnp.exp(cum_end)

    rr = lax.broadcasted_iota(jnp.int32, (C, C), 0)
    cc = lax.broadcasted_iota(jnp.int32, (C, C), 1)
    strict = rr > cc
    lower = rr >= cc

    H = range(RWKV_HEADS)
    hs = [slice(h * RWKV_HEAD, (h + 1) * RWKV_HEAD) for h in H]
    s0 = [state_ref[h] for h in H]
    ar = [jnp.concatenate([a_t[:, sl], r_t[:, sl]], axis=0) for sl in hs]
    bk = [jnp.concatenate([b_t[:, sl], k_t[:, sl]], axis=0) for sl in hs]
    vh = [v[:, sl] for sl in hs]
    gram = [_mm(ar[h], bk[h], 1, 1) for h in H]
    ars0 = [_mm(ar[h], s0[h], 1, 1) for h in H]
    a_ab = [jnp.where(strict, g_[:C, :C], 0.0) for g_ in gram]
    a_ak = [jnp.where(strict, g_[:C, C:], 0.0) for g_ in gram]
    r_abk = [jnp.concatenate([jnp.where(lower, g_[C:, :C], 0.0), jnp.where(lower, g_[C:, C:], 0.0)], axis=1)
             for g_ in gram]
    rhs = [ars0[h][:C] + _mm(a_ak[h], vh[h], 1, 0) for h in H]
    inv = _unit_lower_inverse_many(a_ab, C)
    u = [_mm(inv[h], rhs[h], 1, 0) for h in H]
    uv = [jnp.concatenate([u[h], vh[h]], axis=0) for h in H]
    ys = [ars0[h][C:] + _mm(r_abk[h], uv[h], 1, 0) for h in H]
    bkh = [jnp.concatenate([b_h[:, sl], k_h[:, sl]], axis=0) for sl in hs]
    states = [s0[h] * p_end[:, hs[h]] + _mm(uv[h], bkh[h], 0, 0) for h in H]
    state_ref[...] = jnp.stack(states, axis=0)

    y = jnp.concatenate(ys, axis=1)
    inv_n = 1.0 / RWKV_HEAD
    dev = y - head_sum(y) * inv_n
    var = head_sum(dev * dev) * inv_n
    yn = dev * lax.rsqrt(var + LN_X_EPS) * lng_ref[...] + lnb_ref[...]
    o_ref[...] = ((yn + bonus) * g).astype(o_ref.dtype)

    @pl.when(c == pl.num_programs(1) - 1)
    def _():
        sout_ref[...] = state_ref[...]


def rwkv_mix(z, shift0, s0, prm, *, t_valid):
    bsz, t, cols = z.shape
    C = RWKV_CHUNK
    W = RWKV_WIDTH

    def vec(n):
        return pl.BlockSpec((1, n), lambda b, c: (0, 0))

    def mat(r, n):
        return pl.BlockSpec((r, n), lambda b, c: (0, 0))

    return pl.pallas_call(
        functools.partial(_rwkv_kernel, t_valid=t_valid),
        grid=(bsz, t // C),
        in_specs=[pl.BlockSpec((None, C, cols), lambda b, c: (b, c, 0)),
                  pl.BlockSpec((None, 1, cols), lambda b, c: (b, 0, 0)),
                  pl.BlockSpec((None, RWKV_HEADS, RWKV_HEAD, RWKV_HEAD), lambda b, c: (b, 0, 0, 0)),
                  vec(cols), vec(W), vec(W), vec(W), vec(W), vec(W), vec(W), vec(W),
                  mat(LORA_PAD, W), mat(LORA_PAD, W), mat(GATE_LORA_PAD, W),
                  mat(2 * LANE, 2 * LANE)],
        out_specs=[pl.BlockSpec((None, C, W), lambda b, c: (b, c, 0)),
                   pl.BlockSpec((None, RWKV_HEADS, RWKV_HEAD, RWKV_HEAD), lambda b, c: (b, 0, 0, 0))],
        out_shape=[jax.ShapeDtypeStruct((bsz, t, W), BF16),
                   jax.ShapeDtypeStruct((bsz, RWKV_HEADS, RWKV_HEAD, RWKV_HEAD), F32)],
        scratch_shapes=[pltpu.VMEM((RWKV_HEADS, RWKV_HEAD, RWKV_HEAD), F32),
                        pltpu.VMEM((1, cols), F32)],
        compiler_params=_cparams(("parallel", "arbitrary")),
        name="rwkv_mix",
    )(z, shift0, s0, prm['mu'], prm['w0'], prm['a0'], prm['k_k'], prm['k_a'], prm['r_k'],
      prm['ln_g'], prm['ln_b'], prm['w_up'], prm['a_up'], prm['g_up'],
      prm['bd'])


def _ssd_kernel(z_ref, conv0_ref, h0_ref, cw_ref, cb_ref, dtb_ref, alog_ref, dsk_ref, nrm_ref,
                tri_ref, o_ref, hout_ref, state_ref, tail_ref, *, t_valid):
    Q = SSD_CHUNK
    W = SSM_WIDTH
    c = pl.program_id(1)

    @pl.when(c == 0)
    def _():
        state_ref[...] = h0_ref[...]
        tail_ref[...] = conv0_ref[...]

    zall = z_ref[...]
    zg = zall[:, :W]
    x = zall[:, W:W + SSM_CONV_DIM]
    dt_raw = zall[:, W + SSM_CONV_DIM:]
    tail = tail_ref[...]
    tail_ref[...] = x[Q - CONV_TAIL:, :]

    row = lax.broadcasted_iota(jnp.int32, (Q, 1), 0)
    row_t = lax.broadcasted_iota(jnp.int32, (CONV_TAIL, 1), 0)
    cw = cw_ref[...]
    acc = x * cw[CONV_WIDTH - 1:CONV_WIDTH, :]
    for s in range(1, CONV_WIDTH):
        x_s = pltpu.roll(x, s, axis=0)
        t_s = pltpu.roll(tail, s, axis=0)
        top = jnp.where(row_t < s, t_s, x_s[:CONV_TAIL])
        x_s = jnp.concatenate([top, x_s[CONV_TAIL:]], axis=0)
        acc = acc + x_s * cw[CONV_WIDTH - 1 - s:CONV_WIDTH - s, :]
    xbc = _silu(acc + cb_ref[...])
    xs = xbc[:, :W]
    bm = xbc[:, W:W + SSM_GROUPS * SSM_STATE]
    cm = xbc[:, W + SSM_GROUPS * SSM_STATE:]

    valid = (c * Q + row) < t_valid
    dt = jnp.where(valid, _softplus(dt_raw + dtb_ref[...]), 0.0)
    la = dt * (-jnp.exp(alog_ref[...]))
    cs = _mm_exact_lhs(tri_ref[...], la)
    cs_t = cs.T
    cs_end = cs[Q - 1:Q, :]
    to_end = jnp.exp(cs_end - cs)
    from_start = jnp.exp(cs)
    end_decay = jnp.exp(cs_end)

    rr = lax.broadcasted_iota(jnp.int32, (Q, Q), 0)
    cc = lax.broadcasted_iota(jnp.int32, (Q, Q), 1)
    causal = rr >= cc
    heads_per_group = SSM_HEADS // SSM_GROUPS
    H = range(SSM_HEADS)
    grp = [h // heads_per_group for h in H]
    hsl = [slice(h * SSM_HEAD, (h + 1) * SSM_HEAD) for h in H]
    b_g = [bm[:, g * SSM_STATE:(g + 1) * SSM_STATE] for g in range(SSM_GROUPS)]
    c_g = [cm[:, g * SSM_STATE:(g + 1) * SSM_STATE] for g in range(SSM_GROUPS)]
    cb = [_mm(c_g[g], b_g[g], 1, 1) for g in range(SSM_GROUPS)]
    hs = [state_ref[h] for h in H]
    xh = [xs[:, sl] for sl in hsl]
    xdt = [xh[h] * dt[:, h:h + 1] for h in H]
    decay = [jnp.exp(jnp.where(causal, cs[:, h:h + 1] - cs_t[h:h + 1, :], NEG_INF)) for h in H]
    y_diag = [_mm(cb[grp[h]] * decay[h], xdt[h]) for h in H]
    y_off = [_mm(c_g[grp[h]], hs[h], 1, 1) for h in H]
    upd = [_mm(xdt[h] * to_end[:, h:h + 1], b_g[grp[h]], 0, 0) for h in H]
    ys = [y_diag[h] + y_off[h] * from_start[:, h:h + 1] + dsk_ref[:, hsl[h]] * xh[h] for h in H]
    state_ref[...] = jnp.stack([hs[h] * end_decay[:, h:h + 1] + upd[h] for h in H], axis=0)

    yg = jnp.concatenate(ys, axis=1) * _silu(zg)
    group = W // SSM_GROUPS
    parts = []
    for g in range(SSM_GROUPS):
        t = yg[:, g * group:(g + 1) * group]
        parts.append(t * lax.rsqrt(jnp.mean(t * t, axis=-1, keepdims=True) + NORM_EPS))
    o_ref[...] = (jnp.concatenate(parts, axis=1) * nrm_ref[...]).astype(o_ref.dtype)

    @pl.when(c == pl.num_programs(1) - 1)
    def _():
        hout_ref[...] = state_ref[...]


def ssd_mix(z, conv0, h0, prm, *, t_valid):
    bsz, t, cols = z.shape
    Q = SSD_CHUNK
    W = SSM_WIDTH

    def vec(n):
        return pl.BlockSpec((1, n), lambda b, c: (0, 0))

    state_spec = pl.BlockSpec((None, SSM_HEADS, SSM_HEAD, SSM_STATE), lambda b, c: (b, 0, 0, 0))
    return pl.pallas_call(
        functools.partial(_ssd_kernel, t_valid=t_valid),
        grid=(bsz, t // Q),
        in_specs=[pl.BlockSpec((None, Q, cols), lambda b, c: (b, c, 0)),
                  pl.BlockSpec((None, CONV_TAIL, SSM_CONV_DIM), lambda b, c: (b, 0, 0)),
                  state_spec,
                  pl.BlockSpec((CONV_WIDTH, SSM_CONV_DIM), lambda b, c: (0, 0)),
                  vec(SSM_CONV_DIM), vec(SSM_DT_PAD), vec(SSM_DT_PAD), vec(W), vec(W),
                  pl.BlockSpec((Q, Q), lambda b, c: (0, 0))],
        out_specs=[pl.BlockSpec((None, Q, W), lambda b, c: (b, c, 0)), state_spec],
        out_shape=[jax.ShapeDtypeStruct((bsz, t, W), BF16),
                   jax.ShapeDtypeStruct((bsz, SSM_HEADS, SSM_HEAD, SSM_STATE), F32)],
        scratch_shapes=[pltpu.VMEM((SSM_HEADS, SSM_HEAD, SSM_STATE), F32),
                        pltpu.VMEM((CONV_TAIL, SSM_CONV_DIM), F32)],
        compiler_params=_cparams(("parallel", "arbitrary")),
        name="ssd_mix",
    )(z, conv0, h0, prm['conv_w'], prm['conv_b'], prm['dt_bias'], prm['a_log'], prm['d_skip'],
      prm['ssm_norm'], prm['tri'])


def _topk_block_mask(scores, n_allowed):
    col = lax.broadcasted_iota(jnp.int32, scores.shape, 1).astype(F32)
    sel = jnp.zeros(scores.shape, F32)
    for r in range(MOBA_TOPK):
        top = jnp.max(scores, axis=-1, keepdims=True)
        first = jnp.min(jnp.where(scores == top, col, float(scores.shape[1])), axis=-1, keepdims=True)
        pick = col == first
        sel = jnp.where(pick & (n_allowed > r), 1.0, sel)
        scores = jnp.where(pick, REMOVED, scores)
    return sel


def _attend(q, blocks, carry, scale):
    m, l, acc = carry
    qb = q.astype(BF16)
    ss = [jnp.where(keep, _mm(qb, kb, 1, 1) * scale, NEG_INF) for kb, _, keep in blocks]
    m_new = m
    for s in ss:
        m_new = jnp.maximum(m_new, jnp.max(s, axis=-1, keepdims=True))
    alpha = jnp.exp(m - m_new)
    ps = [jnp.where(keep, jnp.exp(s - m_new), 0.0) for s, (_, _, keep) in zip(ss, blocks)]
    l = l * alpha
    acc = acc * alpha
    for p, (_, vb, _) in zip(ps, blocks):
        l = l + jnp.sum(p, axis=-1, keepdims=True)
        acc = acc + _mm(p, vb)
    return m_new, l, acc


def _block_column(sel, n):
    col = lax.broadcasted_iota(jnp.int32, sel.shape, 1)
    return jnp.sum(jnp.where(col == n, sel, 0.0), axis=-1, keepdims=True) > 0.0


def _moba_prompt_kernel(q_ref, k_ref, v_ref, o_ref, kmean_ref, *, n_blocks):
    BLK = MOBA_BLOCK
    i = pl.program_id(2)
    scale = MOBA_HEAD ** -0.5

    @pl.when(i == 0)
    def _():
        kmean_ref[...] = jnp.zeros_like(kmean_ref)
        for n in range(n_blocks):
            kmean_ref[n:n + 1, :] = jnp.mean(k_ref[n * BLK:(n + 1) * BLK, :], axis=0, keepdims=True)

    q = q_ref[...]
    s_blk = _mm6(q, kmean_ref[...], 1, 1)
    col = lax.broadcasted_iota(jnp.int32, s_blk.shape, 1)
    sel = _topk_block_mask(jnp.where(col < i, s_blk, NEG_INF), i)

    def block(n, keep):
        off = pl.multiple_of(n * BLK, BLK)
        return k_ref[pl.ds(off, BLK), :], v_ref[pl.ds(off, BLK), :], keep

    rr = lax.broadcasted_iota(jnp.int32, (BLK, BLK), 0)
    cc = lax.broadcasted_iota(jnp.int32, (BLK, BLK), 1)
    init = (jnp.full((BLK, 1), NEG_INF, F32), jnp.zeros((BLK, 1), F32), jnp.zeros((BLK, MOBA_HEAD), F32))
    carry = _attend(q, [block(i, rr >= cc)], init, scale)

    def past_group(g, carry):
        base = g * MOBA_GROUP
        return _attend(q, [block(base + j, _block_column(sel, base + j)) for j in range(MOBA_GROUP)],
                       carry, scale)

    n_groups = (i + MOBA_GROUP - 1) // MOBA_GROUP
    _, l, acc = lax.fori_loop(0, n_groups, past_group, carry)
    o_ref[...] = (acc / l).astype(o_ref.dtype)


def moba_prompt(qkv):
    bsz, t, _ = qkv.shape
    n_blocks = t // MOBA_BLOCK
    assert t % MOBA_BLOCK == 0 and n_blocks % MOBA_GROUP == 0, "whole groups of key blocks only"
    q_spec = pl.BlockSpec((None, MOBA_BLOCK, MOBA_HEAD), lambda b, h, i: (b, i, h))
    k_spec = pl.BlockSpec((None, t, MOBA_HEAD), lambda b, h, i: (b, 0, MOBA_HEADS + h))
    v_spec = pl.BlockSpec((None, t, MOBA_HEAD), lambda b, h, i: (b, 0, 2 * MOBA_HEADS + h))
    return pl.pallas_call(
        functools.partial(_moba_prompt_kernel, n_blocks=n_blocks),
        grid=(bsz, MOBA_HEADS, n_blocks),
        in_specs=[q_spec, k_spec, v_spec],
        out_specs=q_spec,
        out_shape=jax.ShapeDtypeStruct((bsz, t, MOBA_WIDTH), BF16),
        scratch_shapes=[pltpu.VMEM((LANE, MOBA_HEAD), F32)],
        compiler_params=_cparams(("parallel", "parallel", "arbitrary")),
        name="moba_prompt",
    )(qkv, qkv, qkv)


SAMPLE_ROWS = 8
QROWS = MOBA_HEADS * SAMPLE_ROWS
PAGES_PER_STEP = 8
PAGES_PER_UPDATE = 4
PAGE_ROWS = PAGE_SIZE * MOBA_HEADS


def _moba_sample_kernel(pt_ref, q_ref, knew_ref, vnew_ref, pick_ref, *rest, n_new, pages_per_block):
    G = PAGES_PER_STEP
    k_refs, v_refs = rest[:G], rest[G:2 * G]
    o_ref, ksum_ref, sel_ref, m_ref, l_ref, acc_ref = rest[2 * G:]
    ph = pl.program_id(1)
    g = pl.program_id(2)
    n_steps = pl.num_programs(2)
    n_blocks = (n_steps * G) // pages_per_block
    blocks_per_step = G // pages_per_block
    scale = MOBA_HEAD ** -0.5
    HEADS = range(MOBA_HEADS)
    hcol = [slice(h * MOBA_HEAD, (h + 1) * MOBA_HEAD) for h in HEADS]
    hrow = [slice(h * SAMPLE_ROWS, (h + 1) * SAMPLE_ROWS) for h in HEADS]
    row_shift = SAMPLE_ROWS.bit_length() - 1

    def own_head(n_cols):
        rr = lax.broadcasted_iota(jnp.int32, (QROWS, n_cols), 0)
        cc = lax.broadcasted_iota(jnp.int32, (QROWS, n_cols), 1)
        return jnp.bitwise_and(cc, MOBA_HEADS - 1) == jnp.right_shift(rr, row_shift)

    def q_rows():
        q = q_ref[...]
        return jnp.concatenate([q[:, hcol[h]] for h in HEADS], axis=0)

    @pl.when(ph == 0)
    def _():
        @pl.when(g == 0)
        def _():
            ksum_ref[...] = jnp.zeros_like(ksum_ref)

        for jb in range(blocks_per_step):
            tot = None
            for jp in range(pages_per_block):
                part = jnp.sum(k_refs[jb * pages_per_block + jp][...], axis=0)
                tot = part if tot is None else tot + part
            ksum_ref[g * blocks_per_step + jb] = tot

    @pl.when((ph == 1) & (g == 0))
    def _():
        qs = q_rows()
        kmean = ksum_ref[...].reshape(LANE * MOBA_HEADS, MOBA_HEAD) * (1.0 / (pages_per_block * PAGE_SIZE))
        s_blk = _mm6(qs, kmean, 1, 1)
        col = lax.broadcasted_iota(jnp.int32, s_blk.shape, 1)
        allowed = own_head(s_blk.shape[1]) & (col < n_blocks * MOBA_HEADS)
        wide = _topk_block_mask(jnp.where(allowed, s_blk, NEG_INF), n_blocks)
        sel_ref[...] = _dg(wide.astype(BF16), pick_ref[...], 1, 0)
        rr = jnp.bitwise_and(lax.broadcasted_iota(jnp.int32, (QROWS, PAGE_SIZE), 0), SAMPLE_ROWS - 1)
        cc = lax.broadcasted_iota(jnp.int32, (QROWS, PAGE_SIZE), 1)
        own_keep = (cc <= rr) & (cc < n_new)
        knew = knew_ref[...]
        vnew = vnew_ref[...]
        s_own = jnp.concatenate([_mm(qs[hrow[h]], knew[:, hcol[h]], 1, 1) for h in HEADS], axis=0) * scale
        s_own = jnp.where(own_keep, s_own, NEG_INF)
        m = jnp.max(s_own, axis=-1, keepdims=True)
        p_own = jnp.where(own_keep, jnp.exp(s_own - m), 0.0)
        m_ref[...] = m
        l_ref[...] = jnp.sum(p_own, axis=-1, keepdims=True)
        acc_ref[...] = jnp.concatenate([_mm(p_own[hrow[h]], vnew[:, hcol[h]]) for h in HEADS], axis=0)

    @pl.when(ph == 1)
    def _():
        qb = q_rows().astype(BF16)
        mine = own_head(PAGE_ROWS)
        sel = sel_ref[...]
        m, l, acc = m_ref[...], l_ref[...], acc_ref[...]
        for j0 in range(0, G, PAGES_PER_UPDATE):
            js = range(j0, j0 + PAGES_PER_UPDATE)
            ks = [k_refs[j][...].reshape(PAGE_ROWS, MOBA_HEAD) for j in js]
            vs = [v_refs[j][...].reshape(PAGE_ROWS, MOBA_HEAD) for j in js]
            keeps = [_block_column(sel, (g * G + j) // pages_per_block) for j in js]
            ss = [jnp.where(keep, jnp.where(mine, _mm(qb, kp, 1, 1) * scale, NEG_INF), NEG_INF)
                  for kp, keep in zip(ks, keeps)]
            m_new = m
            for s in ss:
                m_new = jnp.maximum(m_new, jnp.max(s, axis=-1, keepdims=True))
            alpha = jnp.exp(m - m_new)
            ps = [jnp.exp(s - m_new) for s in ss]
            l = l * alpha
            acc = acc * alpha
            for p, vp in zip(ps, vs):
                l = l + jnp.sum(p, axis=-1, keepdims=True)
                acc = acc + _mm(p, vp)
            m = m_new
        m_ref[...] = m
        l_ref[...] = l
        acc_ref[...] = acc

    @pl.when((ph == 1) & (g == n_steps - 1))
    def _():
        out = acc_ref[...] / l_ref[...]
        for h in HEADS:
            o_ref[:, hcol[h]] = out[hrow[h]]


def moba_sample(q, k_new, v_new, pool_k, pool_v, layer, page_table, *, n_new):
    bsz, n_pages = page_table.shape
    G = PAGES_PER_STEP
    pages_per_block = MOBA_BLOCK // PAGE_SIZE
    assert n_pages % G == 0 and G % pages_per_block == 0 and G % PAGES_PER_UPDATE == 0
    assert n_pages // pages_per_block <= LANE, "block scores live in one lane tile per head"
    n_steps = n_pages // G
    row_spec = pl.BlockSpec((None, SAMPLE_ROWS, MOBA_WIDTH), lambda b, ph, g, pt: (b, 0, 0))
    new_spec = pl.BlockSpec((None, PAGE_SIZE, MOBA_WIDTH), lambda b, ph, g, pt: (b, 0, 0))
    page = (None, None, PAGE_SIZE, MOBA_HEADS, MOBA_HEAD)
    wide = np.arange(LANE * MOBA_HEADS) // MOBA_HEADS
    pick = jnp.asarray((wide[:, None] == np.arange(LANE)[None, :]).astype(np.float32), BF16)

    def k_spec(j):
        return pl.BlockSpec(page, lambda b, ph, g, pt: (layer, pt[b, g * G + j], 0, 0, 0))

    def v_spec(j):
        return pl.BlockSpec(page, lambda b, ph, g, pt: (layer, pt[b, g * ph * G + j], 0, 0, 0))

    grid_spec = pltpu.PrefetchScalarGridSpec(
        num_scalar_prefetch=1,
        grid=(bsz, 2, n_steps),
        in_specs=[row_spec, new_spec, new_spec,
                  pl.BlockSpec((LANE * MOBA_HEADS, LANE), lambda b, ph, g, pt: (0, 0))]
                 + [k_spec(j) for j in range(G)] + [v_spec(j) for j in range(G)],
        out_specs=row_spec,
        scratch_shapes=[pltpu.VMEM((LANE, MOBA_HEADS, MOBA_HEAD), F32),
                        pltpu.VMEM((QROWS, LANE), F32),
                        pltpu.VMEM((QROWS, 1), F32),
                        pltpu.VMEM((QROWS, 1), F32),
                        pltpu.VMEM((QROWS, MOBA_HEAD), F32)])
    return pl.pallas_call(
        functools.partial(_moba_sample_kernel, n_new=n_new, pages_per_block=pages_per_block),
        grid_spec=grid_spec,
        out_shape=jax.ShapeDtypeStruct((bsz, SAMPLE_ROWS, MOBA_WIDTH), F32),
        compiler_params=_cparams(("parallel", "arbitrary", "arbitrary")),
        name="moba_sample",
    )(page_table, q, k_new, v_new, pick, *([pool_k] * G), *([pool_v] * G))


def _cross_kernel(q_ref, k_ref, v_ref, o_ref):
    scale = MEM_HEAD ** -0.5
    for h in range(MEM_HEADS):
        sl = slice(h * MEM_HEAD, (h + 1) * MEM_HEAD)
        s = _mm(q_ref[:, sl], k_ref[:, sl], 1, 1) * scale
        e = jnp.exp(s - jnp.max(s, axis=-1, keepdims=True))
        pr = e / jnp.sum(e, axis=-1, keepdims=True)
        o_ref[:, sl] = _mm(pr, v_ref[:, sl]).astype(o_ref.dtype)


def cross_attention(q, mk, mv, *, tq=512):
    bsz, t, d = q.shape
    n_mem = mk.shape[1]
    tq = _row_tile(t, tq)
    kv_spec = pl.BlockSpec((None, n_mem, d), lambda b, i: (b, 0, 0))
    q_spec = pl.BlockSpec((None, tq, d), lambda b, i: (b, i, 0))
    return pl.pallas_call(
        _cross_kernel,
        grid=(bsz, t // tq),
        in_specs=[q_spec, kv_spec, kv_spec],
        out_specs=q_spec,
        out_shape=jax.ShapeDtypeStruct((bsz, t, d), BF16),
        compiler_params=_cparams(("parallel", "parallel")),
        name="cross_attention",
    )(q, mk, mv)


def _pad_cols(x, n):
    return jnp.pad(x, [(0, 0)] * (x.ndim - 1) + [(0, n - x.shape[-1])])


def _pad_rows(x, n):
    return jnp.pad(x, [(0, n - x.shape[0])] + [(0, 0)] * (x.ndim - 1))


def _rwkv_cols_padded(x):
    W = RWKV_WIDTH
    o = 3 * W
    return jnp.concatenate([
        x[..., :o],
        _pad_cols(x[..., o:o + DECAY_LORA], LORA_PAD),
        _pad_cols(x[..., o + DECAY_LORA:o + DECAY_LORA + AAA_LORA], LORA_PAD),
        _pad_cols(x[..., o + DECAY_LORA + AAA_LORA:], GATE_LORA_PAD)], axis=-1)


def _rwkv_cols_unpadded(x):
    o = 3 * RWKV_WIDTH
    return jnp.concatenate([
        x[..., :o],
        x[..., o:o + DECAY_LORA],
        x[..., o + LORA_PAD:o + LORA_PAD + AAA_LORA],
        x[..., o + 2 * LORA_PAD:o + 2 * LORA_PAD + GATE_LORA]], axis=-1)


def _constants():
    tri_ssd = jnp.asarray(np.tril(np.ones((SSD_CHUNK, SSD_CHUNK), np.float32)), BF16)
    lane_head = np.arange(2 * LANE) // RWKV_HEAD
    same_head = (lane_head[:, None] == lane_head[None, :]).astype(np.float32)
    return tri_ssd, jnp.asarray(same_head, BF16)


def _layer_params(l, P, consts):
    tri_ssd, same_head = consts
    row = lambda x: x.reshape(1, -1).astype(F32)
    w_in = P['w_in'][l]
    o_moba = RWKV_COLS
    o_ssm = o_moba + 3 * MOBA_WIDTH
    o_gate = o_ssm + SSM_COLS
    lp = {}
    for name in ('norm_ffn1', 'norm_ffn2', 'norm_mix', 'norm_cross', 'norm_mem', 'b_gate'):
        lp[name] = row(P[name][l])
    lp['w_rw'] = _rwkv_cols_padded(w_in[:, :RWKV_COLS]).astype(BF16)[None]
    lp['w_qkv'] = w_in[:, o_moba:o_ssm].astype(BF16)[None]
    lp['w_ssm'] = _pad_cols(w_in[:, o_ssm:o_gate], SSM_COLS_PAD).astype(BF16)[None]
    lp['w_gate'] = w_in[:, o_gate:].astype(BF16)[None]
    lp['rw'] = {
        'mu': _rwkv_cols_padded(row(P['rw_mu'][l])),
        'w0': row(P['rw_w0'][l]), 'a0': row(P['rw_a0'][l]),
        'k_k': row(P['rw_k_k'][l]), 'k_a': row(P['rw_k_a'][l]), 'r_k': row(P['rw_r_k'][l]),
        'ln_g': row(P['rw_ln_g'][l]), 'ln_b': row(P['rw_ln_b'][l]),
        'w_up': _pad_rows(P['rw_w_up'][l], LORA_PAD).astype(BF16),
        'a_up': _pad_rows(P['rw_a_up'][l], LORA_PAD).astype(BF16),
        'g_up': _pad_rows(P['rw_g_up'][l], GATE_LORA_PAD).astype(BF16),
        'bd': same_head,
    }
    lp['ssm'] = {
        'conv_w': P['conv_w'][l].astype(F32), 'conv_b': row(P['conv_b'][l]),
        'dt_bias': _pad_cols(row(P['dt_bias'][l]), SSM_DT_PAD),
        'a_log': _pad_cols(row(P['a_log'][l]), SSM_DT_PAD),
        'd_skip': row(jnp.repeat(P['d_skip'][l], SSM_HEAD)),
        'ssm_norm': row(P['ssm_norm'][l]),
        'tri': tri_ssd,
    }
    return lp


def _ffn(x, gain, w_in, w_out, l):
    act = swiglu_in(rms_norm(x, gain, out_dtype=BF16), w_in, l)
    return project_residual(act, w_out, l, x, scale=0.5)


def _pad_time(x, t_pad):
    return jnp.pad(x, ((0, 0), (0, t_pad - x.shape[1]), (0, 0)))


def _trunk_layer(x, l, lp, P, bsz, t, rw_shift0, rw_s0, conv0, ssm_h0, moba_fn, mem_k, mem_v):
    m = bsz * t
    h = _ffn(x, lp['norm_ffn1'], P['w_ffn1_in'], P['w_ffn1_out'], l)

    u = rms_norm(h, lp['norm_mix'], out_dtype=BF16)
    z_rw = project(u, lp['w_rw'], 0)
    qkv = project(u, lp['w_qkv'], 0)
    z_ssm = project(u, lp['w_ssm'], 0, tn=640)
    z_gate = project(u, lp['w_gate'], 0)
    k = qkv[:, MOBA_WIDTH:2 * MOBA_WIDTH]
    v = qkv[:, 2 * MOBA_WIDTH:]

    t_rw = -(-t // RWKV_CHUNK) * RWKV_CHUNK
    z_rw3 = z_rw.reshape(bsz, t, RWKV_COLS_PAD)
    a_rw, rw_s = rwkv_mix(_pad_time(z_rw3, t_rw), _rwkv_cols_padded(rw_shift0)[:, None, :], rw_s0,
                          lp['rw'], t_valid=t)
    a_rw = a_rw[:, :t].reshape(m, RWKV_WIDTH)
    rw_shift = _rwkv_cols_unpadded(z_rw3[:, t - 1])

    a_moba = moba_fn(qkv.reshape(bsz, t, 3 * MOBA_WIDTH)).reshape(m, MOBA_WIDTH)

    t_ssd = -(-t // SSD_CHUNK) * SSD_CHUNK
    z_ssm3 = z_ssm.reshape(bsz, t, SSM_COLS_PAD)
    conv0_pad = jnp.pad(conv0, ((0, 0), (CONV_TAIL - (CONV_WIDTH - 1), 0), (0, 0)))
    a_ssm, ssm_h = ssd_mix(_pad_time(z_ssm3, t_ssd), conv0_pad, ssm_h0, lp['ssm'], t_valid=t)
    a_ssm = a_ssm[:, :t].reshape(m, SSM_WIDTH)
    keep = min(t, CONV_WIDTH - 1)
    xbc_tail = z_ssm3[:, t - keep:, SSM_WIDTH:SSM_WIDTH + SSM_CONV_DIM]
    conv_new = jnp.concatenate([conv0, xbc_tail], axis=1)[:, -(CONV_WIDTH - 1):]

    merged = gated_merge(a_rw, a_moba, a_ssm, P['rw_out'], P['moba_out'], P['ssm_out'], l,
                         z_gate, lp['b_gate'])
    h = project_residual(merged, P['w_mix_out'], l, h, scale=1.0)

    cq = project(rms_norm(h, lp['norm_cross'], out_dtype=BF16), P['w_cq'], l)
    t_ca = -(-t // 8) * 8
    o = cross_attention(_pad_time(cq.reshape(bsz, t, D_MODEL), t_ca), mem_k, mem_v)
    o = o[:, :t].reshape(m, D_MODEL)
    h = project_residual(o, P['w_co'], l, h, scale=1.0)

    h = _ffn(h, lp['norm_ffn2'], P['w_ffn2_in'], P['w_ffn2_out'], l)
    k4 = k.reshape(bsz, t, MOBA_HEADS, MOBA_HEAD)
    v4 = v.reshape(bsz, t, MOBA_HEADS, MOBA_HEAD)
    return h, (k4, v4, rw_s, rw_shift, conv_new, ssm_h)


def _moba_sample_fn(qkv, *, pool_k, pool_v, layer, page_table):
    q, k, v = (qkv[..., i * MOBA_WIDTH:(i + 1) * MOBA_WIDTH] for i in range(3))
    s_new = q.shape[1]
    out = moba_sample(_pad_time(q, SAMPLE_ROWS), _pad_time(k, PAGE_SIZE), _pad_time(v, PAGE_SIZE),
                      pool_k, pool_v, layer, page_table, n_new=s_new)
    return out[:, :s_new].astype(BF16)


def kernel(x_prompt, x_sample, cache_k, cache_v, cache_mem_k, cache_mem_v, state_rwkv, state_rwkv_shift, state_conv, state_ssm, page_table, mem_prompt, norm_ffn1, w_ffn1_in, w_ffn1_out, norm_mix, w_in, rw_mu, rw_w0, rw_w_up, rw_a0, rw_a_up, rw_g_up, rw_k_k, rw_k_a, rw_r_k, rw_ln_g, rw_ln_b, rw_out, moba_out, conv_w, conv_b, dt_bias, a_log, d_skip, ssm_norm, ssm_out, b_gate, w_mix_out, norm_cross, norm_mem, w_cq, w_ckv, w_co, norm_ffn2, w_ffn2_in, w_ffn2_out, norm_final):
    P = dict(norm_ffn1=norm_ffn1, w_ffn1_in=w_ffn1_in, w_ffn1_out=w_ffn1_out, norm_mix=norm_mix,
             w_in=w_in, rw_mu=rw_mu, rw_w0=rw_w0, rw_w_up=rw_w_up, rw_a0=rw_a0, rw_a_up=rw_a_up,
             rw_g_up=rw_g_up, rw_k_k=rw_k_k, rw_k_a=rw_k_a, rw_r_k=rw_r_k, rw_ln_g=rw_ln_g,
             rw_ln_b=rw_ln_b, rw_out=rw_out, moba_out=moba_out, conv_w=conv_w, conv_b=conv_b,
             dt_bias=dt_bias, a_log=a_log, d_skip=d_skip, ssm_norm=ssm_norm, ssm_out=ssm_out,
             b_gate=b_gate, w_mix_out=w_mix_out, norm_cross=norm_cross, norm_mem=norm_mem,
             w_cq=w_cq, w_ckv=w_ckv, w_co=w_co, norm_ffn2=norm_ffn2, w_ffn2_in=w_ffn2_in,
             w_ffn2_out=w_ffn2_out)
    depth = w_in.shape[0]
    bp, tp, d = x_prompt.shape
    bs, ts, _ = x_sample.shape
    n_mem = mem_prompt.shape[1]
    consts = _constants()
    h_p = x_prompt.reshape(bp * tp, d)
    h_s = x_sample.reshape(bs * ts, d)
    mem2 = mem_prompt.reshape(bp * n_mem, d)
    outs_p = [[] for _ in range(8)]
    outs_s = [[] for _ in range(6)]
    for l in range(depth):
        lp = _layer_params(l, P, consts)
        mkv = project(rms_norm(mem2, lp['norm_mem'], out_dtype=BF16), w_ckv, l)
        mk_p = mkv[:, :d].reshape(bp, n_mem, d)
        mv_p = mkv[:, d:].reshape(bp, n_mem, d)
        h_p, st = _trunk_layer(
            h_p, l, lp, P, bp, tp,
            jnp.zeros((bp, RWKV_COLS), F32),
            jnp.zeros((bp, RWKV_HEADS, RWKV_HEAD, RWKV_HEAD), F32),
            jnp.zeros((bp, CONV_WIDTH - 1, SSM_CONV_DIM), F32),
            jnp.zeros((bp, SSM_HEADS, SSM_HEAD, SSM_STATE), F32),
            moba_prompt, mk_p, mv_p)
        for dst, val in zip(outs_p, (st[0], st[1], mk_p.reshape(bp, n_mem, MEM_HEADS, MEM_HEAD),
                                     mv_p.reshape(bp, n_mem, MEM_HEADS, MEM_HEAD),
                                     st[2], st[3], st[4], st[5])):
            dst.append(val)
        moba_fn = functools.partial(_moba_sample_fn, pool_k=cache_k, pool_v=cache_v, layer=l,
                                    page_table=page_table)
        h_s, st = _trunk_layer(
            h_s, l, lp, P, bs, ts, state_rwkv_shift[l], state_rwkv[l], state_conv[l], state_ssm[l],
            moba_fn, cache_mem_k[l].reshape(bs, n_mem, d), cache_mem_v[l].reshape(bs, n_mem, d))
        for dst, val in zip(outs_s, st):
            dst.append(val)
    gain = norm_final.reshape(1, d)
    y_prompt = rms_norm(h_p, gain).reshape(bp, tp, d)
    y_sample = rms_norm(h_s, gain).reshape(bs, ts, d)
    return (y_prompt, y_sample) + tuple(jnp.stack(o) for o in outs_p) + tuple(jnp.stack(o) for o in outs_s)
```

```python
import functools

import jax
import jax.numpy as jnp
import numpy as np
from jax import lax
from jax.experimental import pallas as pl
from jax.experimental.pallas import tpu as pltpu

F32 = jnp.float32
BF16 = jnp.bfloat16

D_MODEL = 2048
NORM_EPS = 1e-6
NEG_INF = -1e30
REMOVED = -3e38

RWKV_WIDTH = 1024
RWKV_HEAD = 64
RWKV_HEADS = 16
DECAY_LORA = 64
AAA_LORA = 64
GATE_LORA = 160
RWKV_COLS = 3 * RWKV_WIDTH + DECAY_LORA + AAA_LORA + GATE_LORA
LN_X_EPS = 64e-5
LORA_PAD = 128
GATE_LORA_PAD = 256
RWKV_COLS_PAD = 3 * RWKV_WIDTH + 2 * LORA_PAD + GATE_LORA_PAD
RWKV_CHUNK = 64
RWKV_INV_BASE = 8

MOBA_WIDTH = 1024
MOBA_HEAD = 128
MOBA_HEADS = 8
MOBA_BLOCK = 256
MOBA_TOPK = 3
MOBA_GROUP = 4
PAGE_SIZE = 128

SSM_WIDTH = 1024
SSM_HEAD = 64
SSM_HEADS = 16
SSM_GROUPS = 4
SSM_STATE = 128
CONV_WIDTH = 4
SSM_CONV_DIM = SSM_WIDTH + 2 * SSM_GROUPS * SSM_STATE
SSM_COLS = SSM_WIDTH + SSM_CONV_DIM + SSM_HEADS
SSM_DT_PAD = 128
SSM_COLS_PAD = SSM_WIDTH + SSM_CONV_DIM + SSM_DT_PAD
SSD_CHUNK = 128
CONV_TAIL = 8

N_BRANCH = 3
MEM_HEADS = 4
MEM_HEAD = 512

LANE = 128
VMEM_LIMIT = 56 * 1024 * 1024


def _cparams(sem):
    return pltpu.CompilerParams(dimension_semantics=sem, vmem_limit_bytes=VMEM_LIMIT)


def _dg(a, b, ca, cb):
    return lax.dot_general(a, b, (((ca,), (cb,)), ((), ())), preferred_element_type=F32)


def _mm(a, b, ca=1, cb=0):
    return _dg(a.astype(BF16), b.astype(BF16), ca, cb)


def _split2(x):
    hi = x.astype(BF16)
    lo = (x - hi.astype(F32)).astype(BF16)
    return hi, lo


def _split3(x):
    hi = x.astype(BF16)
    r = x - hi.astype(F32)
    mid = r.astype(BF16)
    lo = (r - mid.astype(F32)).astype(BF16)
    return hi, mid, lo


def _mm3(a, b, ca=1, cb=0):
    ah, al = _split2(a)
    bh, bl = _split2(b)
    return _dg(ah, bh, ca, cb) + (_dg(ah, bl, ca, cb) + _dg(al, bh, ca, cb))


def _mm6(a, b, ca=1, cb=0):
    a0, a1, a2 = _split3(a)
    b0, b1, b2 = _split3(b)
    small = _dg(a0, b2, ca, cb) + _dg(a2, b0, ca, cb) + _dg(a1, b1, ca, cb)
    mid = _dg(a0, b1, ca, cb) + _dg(a1, b0, ca, cb)
    return _dg(a0, b0, ca, cb) + (mid + small)


def _mm_exact_lhs(lhs01, x):
    x0, x1, x2 = _split3(x)
    return _dg(lhs01, x0, 1, 0) + (_dg(lhs01, x1, 1, 0) + _dg(lhs01, x2, 1, 0))


def _sigmoid(x):
    return 1.0 / (1.0 + jnp.exp(-x))


def _silu(x):
    return x * _sigmoid(x)


def _softplus(x):
    return jnp.maximum(x, 0.0) + jnp.log(1.0 + jnp.exp(-jnp.abs(x)))


def _rms_rows(x, gain):
    ms = jnp.mean(x * x, axis=-1, keepdims=True)
    return x * lax.rsqrt(ms + NORM_EPS) * gain


def _norm_kernel(x_ref, g_ref, o_ref):
    o_ref[...] = _rms_rows(x_ref[...], g_ref[...]).astype(o_ref.dtype)


def _row_tile(m, want):
    return want if m % want == 0 else m


def rms_norm(x, gain, *, out_dtype=F32, tm=512):
    m, k = x.shape
    tm = _row_tile(m, tm)
    return pl.pallas_call(
        _norm_kernel,
        grid=(m // tm,),
        in_specs=[pl.BlockSpec((tm, k), lambda i: (i, 0)),
                  pl.BlockSpec((1, k), lambda i: (0, 0))],
        out_specs=pl.BlockSpec((tm, k), lambda i: (i, 0)),
        out_shape=jax.ShapeDtypeStruct((m, k), out_dtype),
        compiler_params=_cparams(("parallel",)),
        name="rms_norm",
    )(x, gain)


def _stage_weight(w_ref, wb_ref):
    @pl.when(pl.program_id(1) == 0)
    def _():
        wb_ref[...] = w_ref[...].reshape(wb_ref.shape).astype(BF16)


def _proj_kernel(x_ref, w_ref, o_ref, wb_ref):
    _stage_weight(w_ref, wb_ref)
    o_ref[...] = jnp.dot(x_ref[...], wb_ref[...], preferred_element_type=F32).astype(o_ref.dtype)


def _proj_res_kernel(x_ref, w_ref, r_ref, o_ref, wb_ref, *, scale):
    _stage_weight(w_ref, wb_ref)
    x = x_ref[:, :wb_ref.shape[0]]
    o_ref[...] = r_ref[...] + scale * jnp.dot(x, wb_ref[...], preferred_element_type=F32)


def _swiglu_kernel(x_ref, wg_ref, wu_ref, o_ref, wgb_ref, wub_ref, *, last_shift):
    _stage_weight(wg_ref, wgb_ref)
    _stage_weight(wu_ref, wub_ref)
    x = x_ref[...]
    gate = jnp.dot(x, wgb_ref[...], preferred_element_type=F32)
    up = jnp.dot(x, wub_ref[...], preferred_element_type=F32)
    act = (_silu(gate) * up).astype(o_ref.dtype)
    last = pl.num_programs(0) - 1
    tn = act.shape[1]

    @pl.when(pl.program_id(0) < last)
    def _():
        o_ref[...] = act

    @pl.when(pl.program_id(0) == last)
    def _():
        o_ref[:, :tn - last_shift] = act[:, last_shift:]
        if last_shift:
            o_ref[:, tn - last_shift:] = jnp.zeros((act.shape[0], last_shift), o_ref.dtype)


def _layer_cols(l, k, tn):
    return pl.BlockSpec((None, k, tn), lambda j, i: (l, 0, j))


def _col_tile(n):
    return next(t for t in (1024, 640, 512) if n % t == 0)


def project(x, w, l, *, tm=1024, out_dtype=F32):
    m, k = x.shape
    n = w.shape[2]
    tn = _col_tile(n)
    tm = _row_tile(m, tm)
    return pl.pallas_call(
        _proj_kernel,
        grid=(n // tn, m // tm),
        in_specs=[pl.BlockSpec((tm, k), lambda j, i: (i, 0)), _layer_cols(l, k, tn)],
        out_specs=pl.BlockSpec((tm, tn), lambda j, i: (i, j)),
        out_shape=jax.ShapeDtypeStruct((m, n), out_dtype),
        scratch_shapes=[pltpu.VMEM((k, tn), BF16)],
        compiler_params=_cparams(("parallel", "arbitrary")),
        name="project",
    )(x, w)


def project_residual(x, w, l, res, *, scale):
    m, kx = x.shape
    k, n = w.shape[1:]
    tm, tn = (1024, 1024) if k <= D_MODEL else (512, 512)
    tm = _row_tile(m, tm)
    tile = pl.BlockSpec((tm, tn), lambda j, i: (i, j))
    return pl.pallas_call(
        functools.partial(_proj_res_kernel, scale=scale),
        grid=(n // tn, m // tm),
        in_specs=[pl.BlockSpec((tm, kx), lambda j, i: (i, 0)), _layer_cols(l, k, tn), tile],
        out_specs=tile,
        out_shape=jax.ShapeDtypeStruct((m, n), F32),
        scratch_shapes=[pltpu.VMEM((k, tn), BF16)],
        compiler_params=_cparams(("parallel", "arbitrary")),
        name="project_residual",
    )(x, w, res)


def swiglu_in(x, w_in, l, *, tn=512, tm=1024):
    m, k = x.shape
    f = w_in.shape[2] // 2
    tm = _row_tile(m, tm)
    n_tiles = -(-f // tn)
    last_shift = n_tiles * tn - f
    assert last_shift % LANE == 0 and f >= tn

    def w_spec(base):
        return pl.BlockSpec((pl.Element(1), pl.Element(k), pl.Element(tn)),
                            lambda j, i: (l, 0, pl.multiple_of(base + jnp.minimum(j * tn, f - tn), LANE)))

    return pl.pallas_call(
        functools.partial(_swiglu_kernel, last_shift=last_shift),
        grid=(n_tiles, m // tm),
        in_specs=[pl.BlockSpec((tm, k), lambda j, i: (i, 0)), w_spec(0), w_spec(f)],
        out_specs=pl.BlockSpec((tm, tn), lambda j, i: (i, j)),
        out_shape=jax.ShapeDtypeStruct((m, n_tiles * tn), BF16),
        scratch_shapes=[pltpu.VMEM((k, tn), BF16), pltpu.VMEM((k, tn), BF16)],
        compiler_params=_cparams(("parallel", "arbitrary")),
        name="swiglu_in",
    )(x, w_in, w_in)


def _merge_kernel(arw_ref, amo_ref, ass_ref, wrw_ref, wmo_ref, wss_ref,
                  g0_ref, g1_ref, g2_ref, b0_ref, b1_ref, b2_ref, o_ref, wrwb_ref, wmob_ref, wssb_ref):
    _stage_weight(wrw_ref, wrwb_ref)
    _stage_weight(wmo_ref, wmob_ref)
    _stage_weight(wss_ref, wssb_ref)
    y_rw = jnp.dot(arw_ref[...], wrwb_ref[...], preferred_element_type=F32)
    y_mo = jnp.dot(amo_ref[...], wmob_ref[...], preferred_element_type=F32)
    y_ss = jnp.dot(ass_ref[...], wssb_ref[...], preferred_element_type=F32)
    merged = (_sigmoid(g0_ref[...] + b0_ref[...]) * y_rw
              + _sigmoid(g1_ref[...] + b1_ref[...]) * y_mo
              + _sigmoid(g2_ref[...] + b2_ref[...]) * y_ss)
    o_ref[...] = merged.astype(o_ref.dtype)


def gated_merge(a_rw, a_moba, a_ssm, w_rw, w_moba, w_ssm, l, z_gate, b_gate, *, tm=512, tn=512):
    m, k = a_rw.shape
    n = w_rw.shape[2]
    tm = _row_tile(m, tm)
    nj = n // tn
    a_spec = pl.BlockSpec((tm, k), lambda j, i: (i, 0))
    w_spec = _layer_cols(l, k, tn)

    def gate_spec(br):
        return pl.BlockSpec((tm, tn), lambda j, i: (i, br * nj + j))

    def bias_spec(br):
        return pl.BlockSpec((1, tn), lambda j, i: (0, br * nj + j))

    return pl.pallas_call(
        _merge_kernel,
        grid=(nj, m // tm),
        in_specs=[a_spec, a_spec, a_spec, w_spec, w_spec, w_spec,
                  gate_spec(0), gate_spec(1), gate_spec(2),
                  bias_spec(0), bias_spec(1), bias_spec(2)],
        out_specs=pl.BlockSpec((tm, tn), lambda j, i: (i, j)),
        out_shape=jax.ShapeDtypeStruct((m, n), BF16),
        scratch_shapes=[pltpu.VMEM((k, tn), BF16)] * 3,
        compiler_params=_cparams(("parallel", "arbitrary")),
        name="gated_merge",
    )(a_rw, a_moba, a_ssm, w_rw, w_moba, w_ssm, z_gate, z_gate, z_gate, b_gate, b_gate, b_gate)


def _unit_lower_inverse_many(lows, n):
    row = lax.broadcasted_iota(jnp.int32, (n, n), 0)
    col = lax.broadcasted_iota(jnp.int32, (n, n), 1)
    eye = (row == col).astype(F32)
    size = RWKV_INV_BASE

    def same_block(width):
        sh = width.bit_length() - 1
        return jnp.right_shift(row, sh) == jnp.right_shift(col, sh)

    same = same_block(size)
    power = [jnp.where(same, low, 0.0) for low in lows]
    inv = [eye + p for p in power]
    span = 2
    while span < size:
        power = [_mm3(p, p, 1, 0) for p in power]
        inv = [t + _mm3(t, p, 1, 0) for t, p in zip(inv, power)]
        span *= 2
    while size < n:
        same_next = same_block(2 * size)
        pick = same_next & jnp.logical_not(same)
        tmp = [_mm3(t, jnp.where(pick, low, 0.0), 1, 0) for t, low in zip(inv, lows)]
        inv = [t + _mm3(x, t, 1, 0) for t, x in zip(inv, tmp)]
        same = same_next
        size *= 2
    return inv


def _rwkv_kernel(z_ref, sh0_ref, s0_ref, mu_ref, w0_ref, a0_ref, kk_ref, ka_ref, rk_ref,
                 lng_ref, lnb_ref, wup_ref, aup_ref, gup_ref, bd_ref,
                 o_ref, sout_ref, state_ref, prev_ref, *, t_valid):
    C = RWKV_CHUNK
    W = RWKV_WIDTH
    c = pl.program_id(1)

    @pl.when(c == 0)
    def _():
        state_ref[...] = s0_ref[...]
        prev_ref[...] = sh0_ref[...]

    z = z_ref[...]
    row = lax.broadcasted_iota(jnp.int32, (C, 1), 0)
    valid = (c * C + row) < t_valid
    z_prev = jnp.where(row == 0, prev_ref[...], pltpu.roll(z, 1, axis=0))
    prev_ref[...] = z[C - 1:C, :]
    zm = z + (z_prev - z) * mu_ref[...]
    zm = jnp.where(valid, zm, 0.0)
    r = zm[:, 0:W]
    k = zm[:, W:2 * W]
    v = zm[:, 2 * W:3 * W]
    w_lo = zm[:, 3 * W:3 * W + LORA_PAD]
    a_lo = zm[:, 3 * W + LORA_PAD:3 * W + 2 * LORA_PAD]
    g_lo = zm[:, 3 * W + 2 * LORA_PAD:]

    w = -_softplus(-(w0_ref[...] + _mm(jnp.tanh(w_lo), wup_ref[...]))) - 0.5
    logd = jnp.where(valid, -jnp.exp(w), 0.0)
    a = _sigmoid(a0_ref[...] + _mm(a_lo, aup_ref[...]))
    g = _mm(_sigmoid(g_lo), gup_ref[...])

    bd = bd_ref[...]
    n_grp = W // (2 * LANE)

    def head_sum(x):
        parts = _split2(x)
        rows = jnp.concatenate([p[:, g_ * 2 * LANE:(g_ + 1) * 2 * LANE] for p in parts for g_ in range(n_grp)],
                               axis=0)
        res = _dg(rows, bd, 1, 0)
        tot = res[:n_grp * C] + res[n_grp * C:]
        return jnp.concatenate([tot[g_ * C:(g_ + 1) * C] for g_ in range(n_grp)], axis=1)

    kkr = k * kk_ref[...]
    kk = kkr * lax.rsqrt(jnp.maximum(head_sum(kkr * kkr), 1e-12))
    k2 = k * (1.0 + (a - 1.0) * ka_ref[...])
    b = kk * a
    bonus = head_sum(r * k2 * rk_ref[...]) * v

    cum = logd
    shift = 1
    while shift < C:
        cum = cum + jnp.where(row >= shift, pltpu.roll(cum, shift, axis=0), 0.0)
        shift *= 2
    cum_end = cum[C - 1:C, :]
    e_neg = jnp.exp(-cum)
    e_end = jnp.exp(cum_end - cum)
    a_t = -kk * jnp.exp(cum - logd)
    r_t = r * jnp.exp(cum)
    b_t = b * e_neg
    k_t = k2 * e_neg
    b_h = b * e_end
    k_h = k2 * e_end
    p_end = jnp.exp(cum_end)

    rr = lax.broadcasted_iota(jnp.int32, (C, C), 0)
    cc = lax.broadcasted_iota(jnp.int32, (C, C), 1)
    strict = rr > cc
    lower = rr >= cc

    H = range(RWKV_HEADS)
    hs = [slice(h * RWKV_HEAD, (h + 1) * RWKV_HEAD) for h in H]
    s0 = [state_ref[h] for h in H]
    ar = [jnp.concatenate([a_t[:, sl], r_t[:, sl]], axis=0) for sl in hs]
    bk = [jnp.concatenate([b_t[:, sl], k_t[:, sl]], axis=0) for sl in hs]
    vh = [v[:, sl] for sl in hs]
    gram = [_mm(ar[h], bk[h], 1, 1) for h in H]
    ars0 = [_mm(ar[h], s0[h], 1, 1) for h in H]
    a_ab = [jnp.where(strict, g_[:C, :C], 0.0) for g_ in gram]
    a_ak = [jnp.where(strict, g_[:C, C:], 0.0) for g_ in gram]
    r_abk = [jnp.concatenate([jnp.where(lower, g_[C:, :C], 0.0), jnp.where(lower, g_[C:, C:], 0.0)], axis=1)
             for g_ in gram]
    rhs = [ars0[h][:C] + _mm(a_ak[h], vh[h], 1, 0) for h in H]
    inv = _unit_lower_inverse_many(a_ab, C)
    u = [_mm(inv[h], rhs[h], 1, 0) for h in H]
    uv = [jnp.concatenate([u[h], vh[h]], axis=0) for h in H]
    ys = [ars0[h][C:] + _mm(r_abk[h], uv[h], 1, 0) for h in H]
    bkh = [jnp.concatenate([b_h[:, sl], k_h[:, sl]], axis=0) for sl in hs]
    states = [s0[h] * p_end[:, hs[h]] + _mm(uv[h], bkh[h], 0, 0) for h in H]
    state_ref[...] = jnp.stack(states, axis=0)

    y = jnp.concatenate(ys, axis=1)
    inv_n = 1.0 / RWKV_HEAD
    dev = y - head_sum(y) * inv_n
    var = head_sum(dev * dev) * inv_n
    yn = dev * lax.rsqrt(var + LN_X_EPS) * lng_ref[...] + lnb_ref[...]
    o_ref[...] = ((yn + bonus) * g).astype(o_ref.dtype)

    @pl.when(c == pl.num_programs(1) - 1)
    def _():
        sout_ref[...] = state_ref[...]


def rwkv_mix(z, shift0, s0, prm, *, t_valid):
    bsz, t, cols = z.shape
    C = RWKV_CHUNK
    W = RWKV_WIDTH

    def vec(n):
        return pl.BlockSpec((1, n), lambda b, c: (0, 0))

    def mat(r, n):
        return pl.BlockSpec((r, n), lambda b, c: (0, 0))

    return pl.pallas_call(
        functools.partial(_rwkv_kernel, t_valid=t_valid),
        grid=(bsz, t // C),
        in_specs=[pl.BlockSpec((None, C, cols), lambda b, c: (b, c, 0)),
                  pl.BlockSpec((None, 1, cols), lambda b, c: (b, 0, 0)),
                  pl.BlockSpec((None, RWKV_HEADS, RWKV_HEAD, RWKV_HEAD), lambda b, c: (b, 0, 0, 0)),
                  vec(cols), vec(W), vec(W), vec(W), vec(W), vec(W), vec(W), vec(W),
                  mat(LORA_PAD, W), mat(LORA_PAD, W), mat(GATE_LORA_PAD, W),
                  mat(2 * LANE, 2 * LANE)],
        out_specs=[pl.BlockSpec((None, C, W), lambda b, c: (b, c, 0)),
                   pl.BlockSpec((None, RWKV_HEADS, RWKV_HEAD, RWKV_HEAD), lambda b, c: (b, 0, 0, 0))],
        out_shape=[jax.ShapeDtypeStruct((bsz, t, W), BF16),
                   jax.ShapeDtypeStruct((bsz, RWKV_HEADS, RWKV_HEAD, RWKV_HEAD), F32)],
        scratch_shapes=[pltpu.VMEM((RWKV_HEADS, RWKV_HEAD, RWKV_HEAD), F32),
                        pltpu.VMEM((1, cols), F32)],
        compiler_params=_cparams(("parallel", "arbitrary")),
        name="rwkv_mix",
    )(z, shift0, s0, prm['mu'], prm['w0'], prm['a0'], prm['k_k'], prm['k_a'], prm['r_k'],
      prm['ln_g'], prm['ln_b'], prm['w_up'], prm['a_up'], prm['g_up'],
      prm['bd'])


def _ssd_kernel(z_ref, conv0_ref, h0_ref, cw_ref, cb_ref, dtb_ref, alog_ref, dsk_ref, nrm_ref,
                tri_ref, o_ref, hout_ref, state_ref, tail_ref, *, t_valid):
    Q = SSD_CHUNK
    W = SSM_WIDTH
    c = pl.program_id(1)

    @pl.when(c == 0)
    def _():
        state_ref[...] = h0_ref[...]
        tail_ref[...] = conv0_ref[...]

    zall = z_ref[...]
    zg = zall[:, :W]
    x = zall[:, W:W + SSM_CONV_DIM]
    dt_raw = zall[:, W + SSM_CONV_DIM:]
    tail = tail_ref[...]
    tail_ref[...] = x[Q - CONV_TAIL:, :]

    row = lax.broadcasted_iota(jnp.int32, (Q, 1), 0)
    row_t = lax.broadcasted_iota(jnp.int32, (CONV_TAIL, 1), 0)
    cw = cw_ref[...]
    acc = x * cw[CONV_WIDTH - 1:CONV_WIDTH, :]
    for s in range(1, CONV_WIDTH):
        x_s = pltpu.roll(x, s, axis=0)
        t_s = pltpu.roll(tail, s, axis=0)
        top = jnp.where(row_t < s, t_s, x_s[:CONV_TAIL])
        x_s = jnp.concatenate([top, x_s[CONV_TAIL:]], axis=0)
        acc = acc + x_s * cw[CONV_WIDTH - 1 - s:CONV_WIDTH - s, :]
    xbc = _silu(acc + cb_ref[...])
    xs = xbc[:, :W]
    bm = xbc[:, W:W + SSM_GROUPS * SSM_STATE]
    cm = xbc[:, W + SSM_GROUPS * SSM_STATE:]

    valid = (c * Q + row) < t_valid
    dt = jnp.where(valid, _softplus(dt_raw + dtb_ref[...]), 0.0)
    la = dt * (-jnp.exp(alog_ref[...]))
    cs = _mm_exact_lhs(tri_ref[...], la)
    cs_t = cs.T
    cs_end = cs[Q - 1:Q, :]
    to_end = jnp.exp(cs_end - cs)
    from_start = jnp.exp(cs)
    end_decay = jnp.exp(cs_end)

    rr = lax.broadcasted_iota(jnp.int32, (Q, Q), 0)
    cc = lax.broadcasted_iota(jnp.int32, (Q, Q), 1)
    causal = rr >= cc
    heads_per_group = SSM_HEADS // SSM_GROUPS
    H = range(SSM_HEADS)
    grp = [h // heads_per_group for h in H]
    hsl = [slice(h * SSM_HEAD, (h + 1) * SSM_HEAD) for h in H]
    b_g = [bm[:, g * SSM_STATE:(g + 1) * SSM_STATE] for g in range(SSM_GROUPS)]
    c_g = [cm[:, g * SSM_STATE:(g + 1) * SSM_STATE] for g in range(SSM_GROUPS)]
    cb = [_mm(c_g[g], b_g[g], 1, 1) for g in range(SSM_GROUPS)]
    hs = [state_ref[h] for h in H]
    xh = [xs[:, sl] for sl in hsl]
    xdt = [xh[h] * dt[:, h:h + 1] for h in H]
    decay = [jnp.exp(jnp.where(causal, cs[:, h:h + 1] - cs_t[h:h + 1, :], NEG_INF)) for h in H]
    y_diag = [_mm(cb[grp[h]] * decay[h], xdt[h]) for h in H]
    y_off = [_mm(c_g[grp[h]], hs[h], 1, 1) for h in H]
    upd = [_mm(xdt[h] * to_end[:, h:h + 1], b_g[grp[h]], 0, 0) for h in H]
    ys = [y_diag[h] + y_off[h] * from_start[:, h:h + 1] + dsk_ref[:, hsl[h]] * xh[h] for h in H]
    state_ref[...] = jnp.stack([hs[h] * end_decay[:, h:h + 1] + upd[h] for h in H], axis=0)

    yg = jnp.concatenate(ys, axis=1) * _silu(zg)
    group = W // SSM_GROUPS
    parts = []
    for g in range(SSM_GROUPS):
        t = yg[:, g * group:(g + 1) * group]
        parts.append(t * lax.rsqrt(jnp.mean(t * t, axis=-1, keepdims=True) + NORM_EPS))
    o_ref[...] = (jnp.concatenate(parts, axis=1) * nrm_ref[...]).astype(o_ref.dtype)

    @pl.when(c == pl.num_programs(1) - 1)
    def _():
        hout_ref[...] = state_ref[...]


def ssd_mix(z, conv0, h0, prm, *, t_valid):
    bsz, t, cols = z.shape
    Q = SSD_CHUNK
    W = SSM_WIDTH

    def vec(n):
        return pl.BlockSpec((1, n), lambda b, c: (0, 0))

    state_spec = pl.BlockSpec((None, SSM_HEADS, SSM_HEAD, SSM_STATE), lambda b, c: (b, 0, 0, 0))
    return pl.pallas_call(
        functools.partial(_ssd_kernel, t_valid=t_valid),
        grid=(bsz, t // Q),
        in_specs=[pl.BlockSpec((None, Q, cols), lambda b, c: (b, c, 0)),
                  pl.BlockSpec((None, CONV_TAIL, SSM_CONV_DIM), lambda b, c: (b, 0, 0)),
                  state_spec,
                  pl.BlockSpec((CONV_WIDTH, SSM_CONV_DIM), lambda b, c: (0, 0)),
                  vec(SSM_CONV_DIM), vec(SSM_DT_PAD), vec(SSM_DT_PAD), vec(W), vec(W),
                  pl.BlockSpec((Q, Q), lambda b, c: (0, 0))],
        out_specs=[pl.BlockSpec((None, Q, W), lambda b, c: (b, c, 0)), state_spec],
        out_shape=[jax.ShapeDtypeStruct((bsz, t, W), BF16),
                   jax.ShapeDtypeStruct((bsz, SSM_HEADS, SSM_HEAD, SSM_STATE), F32)],
        scratch_shapes=[pltpu.VMEM((SSM_HEADS, SSM_HEAD, SSM_STATE), F32),
                        pltpu.VMEM((CONV_TAIL, SSM_CONV_DIM), F32)],
        compiler_params=_cparams(("parallel", "arbitrary")),
        name="ssd_mix",
    )(z, conv0, h0, prm['conv_w'], prm['conv_b'], prm['dt_bias'], prm['a_log'], prm['d_skip'],
      prm['ssm_norm'], prm['tri'])


def _topk_block_mask(scores, n_allowed):
    col = lax.broadcasted_iota(jnp.int32, scores.shape, 1).astype(F32)
    sel = jnp.zeros(scores.shape, F32)
    for r in range(MOBA_TOPK):
        top = jnp.max(scores, axis=-1, keepdims=True)
        first = jnp.min(jnp.where(scores == top, col, float(scores.shape[1])), axis=-1, keepdims=True)
        pick = col == first
        sel = jnp.where(pick & (n_allowed > r), 1.0, sel)
        scores = jnp.where(pick, REMOVED, scores)
    return sel


def _attend(q, blocks, carry, scale):
    m, l, acc = carry
    qb = q.astype(BF16)
    ss = [jnp.where(keep, _mm(qb, kb, 1, 1) * scale, NEG_INF) for kb, _, keep in blocks]
    m_new = m
    for s in ss:
        m_new = jnp.maximum(m_new, jnp.max(s, axis=-1, keepdims=True))
    alpha = jnp.exp(m - m_new)
    ps = [jnp.exp(s - m_new) for s in ss]
    l = l * alpha
    acc = acc * alpha
    for p, (_, vb, _) in zip(ps, blocks):
        l = l + jnp.sum(p, axis=-1, keepdims=True)
        acc = acc + _mm(p, vb)
    return m_new, l, acc


def _topk_rows_mask(scores, n_allowed):
    row = lax.broadcasted_iota(jnp.int32, scores.shape, 0).astype(F32)
    sel = jnp.zeros(scores.shape, F32)
    for r in range(MOBA_TOPK):
        top = jnp.max(scores, axis=0, keepdims=True)
        first = jnp.min(jnp.where(scores == top, row, float(scores.shape[0])), axis=0, keepdims=True)
        pick = row == first
        sel = jnp.where(pick & (n_allowed > r), 1.0, sel)
        scores = jnp.where(pick, REMOVED, scores)
    return sel


def _block_column(sel, n):
    col = lax.broadcasted_iota(jnp.int32, sel.shape, 1)
    return jnp.sum(jnp.where(col == n, sel, 0.0), axis=-1, keepdims=True) > 0.0


def _moba_prompt_kernel(q_ref, k_ref, v_ref, o_ref, kmean_ref, *, n_blocks):
    BLK = MOBA_BLOCK
    i = pl.program_id(2)
    scale = MOBA_HEAD ** -0.5

    @pl.when(i == 0)
    def _():
        kmean_ref[...] = jnp.zeros_like(kmean_ref)
        for n in range(n_blocks):
            kmean_ref[n:n + 1, :] = jnp.mean(k_ref[n * BLK:(n + 1) * BLK, :], axis=0, keepdims=True)

    q = q_ref[...]
    s_blk = _mm6(kmean_ref[...], q, 1, 1)
    row = lax.broadcasted_iota(jnp.int32, s_blk.shape, 0)
    sel = _topk_rows_mask(jnp.where(row < i, s_blk, NEG_INF), i).T

    def block(n, keep):
        off = pl.multiple_of(n * BLK, BLK)
        return k_ref[pl.ds(off, BLK), :], v_ref[pl.ds(off, BLK), :], keep

    rr = lax.broadcasted_iota(jnp.int32, (BLK, BLK), 0)
    cc = lax.broadcasted_iota(jnp.int32, (BLK, BLK), 1)
    init = (jnp.full((BLK, 1), NEG_INF, F32), jnp.zeros((BLK, 1), F32), jnp.zeros((BLK, MOBA_HEAD), F32))
    carry = _attend(q, [block(i, rr >= cc)], init, scale)

    def past_group(g, carry):
        base = g * MOBA_GROUP
        return _attend(q, [block(base + j, _block_column(sel, base + j)) for j in range(MOBA_GROUP)],
                       carry, scale)

    n_groups = (i + MOBA_GROUP - 1) // MOBA_GROUP
    _, l, acc = lax.fori_loop(0, n_groups, past_group, carry)
    o_ref[...] = (acc / l).astype(o_ref.dtype)


def moba_prompt(qkv):
    bsz, t, _ = qkv.shape
    n_blocks = t // MOBA_BLOCK
    assert t % MOBA_BLOCK == 0 and n_blocks % MOBA_GROUP == 0, "whole groups of key blocks only"
    q_spec = pl.BlockSpec((None, MOBA_BLOCK, MOBA_HEAD), lambda b, h, i: (b, i, h))
    k_spec = pl.BlockSpec((None, t, MOBA_HEAD), lambda b, h, i: (b, 0, MOBA_HEADS + h))
    v_spec = pl.BlockSpec((None, t, MOBA_HEAD), lambda b, h, i: (b, 0, 2 * MOBA_HEADS + h))
    return pl.pallas_call(
        functools.partial(_moba_prompt_kernel, n_blocks=n_blocks),
        grid=(bsz, MOBA_HEADS, n_blocks),
        in_specs=[q_spec, k_spec, v_spec],
        out_specs=q_spec,
        out_shape=jax.ShapeDtypeStruct((bsz, t, MOBA_WIDTH), BF16),
        scratch_shapes=[pltpu.VMEM((LANE, MOBA_HEAD), F32)],
        compiler_params=_cparams(("parallel", "parallel", "arbitrary")),
        name="moba_prompt",
    )(qkv, qkv, qkv)


SAMPLE_ROWS = 8
QROWS = MOBA_HEADS * SAMPLE_ROWS
PAGES_PER_STEP = 8
PAGES_PER_UPDATE = 4
PAGE_ROWS = PAGE_SIZE * MOBA_HEADS


def _moba_sample_kernel(pt_ref, q_ref, knew_ref, vnew_ref, pick_ref, *rest, n_new, pages_per_block):
    G = PAGES_PER_STEP
    k_refs, v_refs = rest[:G], rest[G:2 * G]
    o_ref, ksum_ref, sel_ref, m_ref, l_ref, acc_ref = rest[2 * G:]
    ph = pl.program_id(1)
    g = pl.program_id(2)
    n_steps = pl.num_programs(2)
    n_blocks = (n_steps * G) // pages_per_block
    blocks_per_step = G // pages_per_block
    scale = MOBA_HEAD ** -0.5
    HEADS = range(MOBA_HEADS)
    hcol = [slice(h * MOBA_HEAD, (h + 1) * MOBA_HEAD) for h in HEADS]
    hrow = [slice(h * SAMPLE_ROWS, (h + 1) * SAMPLE_ROWS) for h in HEADS]
    row_shift = SAMPLE_ROWS.bit_length() - 1

    def own_head(n_cols):
        rr = lax.broadcasted_iota(jnp.int32, (QROWS, n_cols), 0)
        cc = lax.broadcasted_iota(jnp.int32, (QROWS, n_cols), 1)
        return jnp.bitwise_and(cc, MOBA_HEADS - 1) == jnp.right_shift(rr, row_shift)

    def q_rows():
        q = q_ref[...]
        return jnp.concatenate([q[:, hcol[h]] for h in HEADS], axis=0)

    @pl.when(ph == 0)
    def _():
        @pl.when(g == 0)
        def _():
            ksum_ref[...] = jnp.zeros_like(ksum_ref)

        for jb in range(blocks_per_step):
            tot = None
            for jp in range(pages_per_block):
                part = jnp.sum(k_refs[jb * pages_per_block + jp][...], axis=0)
                tot = part if tot is None else tot + part
            ksum_ref[g * blocks_per_step + jb] = tot

    @pl.when((ph == 1) & (g == 0))
    def _():
        qs = q_rows()
        kmean = ksum_ref[...].reshape(LANE * MOBA_HEADS, MOBA_HEAD) * (1.0 / (pages_per_block * PAGE_SIZE))
        s_blk = _mm6(qs, kmean, 1, 1)
        col = lax.broadcasted_iota(jnp.int32, s_blk.shape, 1)
        allowed = own_head(s_blk.shape[1]) & (col < n_blocks * MOBA_HEADS)
        wide = _topk_block_mask(jnp.where(allowed, s_blk, NEG_INF), n_blocks)
        sel_ref[...] = _dg(wide.astype(BF16), pick_ref[...], 1, 0)
        rr = jnp.bitwise_and(lax.broadcasted_iota(jnp.int32, (QROWS, PAGE_SIZE), 0), SAMPLE_ROWS - 1)
        cc = lax.broadcasted_iota(jnp.int32, (QROWS, PAGE_SIZE), 1)
        own_keep = (cc <= rr) & (cc < n_new)
        knew = knew_ref[...]
        vnew = vnew_ref[...]
        s_own = jnp.concatenate([_mm(qs[hrow[h]], knew[:, hcol[h]], 1, 1) for h in HEADS], axis=0) * scale
        s_own = jnp.where(own_keep, s_own, NEG_INF)
        m = jnp.max(s_own, axis=-1, keepdims=True)
        p_own = jnp.where(own_keep, jnp.exp(s_own - m), 0.0)
        m_ref[...] = m
        l_ref[...] = jnp.sum(p_own, axis=-1, keepdims=True)
        acc_ref[...] = jnp.concatenate([_mm(p_own[hrow[h]], vnew[:, hcol[h]]) for h in HEADS], axis=0)

    @pl.when(ph == 1)
    def _():
        qb = q_rows().astype(BF16)
        mine = own_head(PAGE_ROWS)
        sel = sel_ref[...]
        m, l, acc = m_ref[...], l_ref[...], acc_ref[...]
        for j0 in range(0, G, PAGES_PER_UPDATE):
            js = range(j0, j0 + PAGES_PER_UPDATE)
            ks = [k_refs[j][...].reshape(PAGE_ROWS, MOBA_HEAD) for j in js]
            vs = [v_refs[j][...].reshape(PAGE_ROWS, MOBA_HEAD) for j in js]
            keeps = [_block_column(sel, (g * G + j) // pages_per_block) for j in js]
            ss = [jnp.where(keep, jnp.where(mine, _mm(qb, kp, 1, 1) * scale, NEG_INF), NEG_INF)
                  for kp, keep in zip(ks, keeps)]
            m_new = m
            for s in ss:
                m_new = jnp.maximum(m_new, jnp.max(s, axis=-1, keepdims=True))
            alpha = jnp.exp(m - m_new)
            ps = [jnp.exp(s - m_new) for s in ss]
            l = l * alpha
            acc = acc * alpha
            for p, vp in zip(ps, vs):
                l = l + jnp.sum(p, axis=-1, keepdims=True)
                acc = acc + _mm(p, vp)
            m = m_new
        m_ref[...] = m
        l_ref[...] = l
        acc_ref[...] = acc

    @pl.when((ph == 1) & (g == n_steps - 1))
    def _():
        out = acc_ref[...] / l_ref[...]
        for h in HEADS:
            o_ref[:, hcol[h]] = out[hrow[h]]


def moba_sample(q, k_new, v_new, pool_k, pool_v, layer, page_table, *, n_new):
    bsz, n_pages = page_table.shape
    G = PAGES_PER_STEP
    pages_per_block = MOBA_BLOCK // PAGE_SIZE
    assert n_pages % G == 0 and G % pages_per_block == 0 and G % PAGES_PER_UPDATE == 0
    assert n_pages // pages_per_block <= LANE, "block scores live in one lane tile per head"
    n_steps = n_pages // G
    row_spec = pl.BlockSpec((None, SAMPLE_ROWS, MOBA_WIDTH), lambda b, ph, g, pt: (b, 0, 0))
    new_spec = pl.BlockSpec((None, PAGE_SIZE, MOBA_WIDTH), lambda b, ph, g, pt: (b, 0, 0))
    page = (None, None, PAGE_SIZE, MOBA_HEADS, MOBA_HEAD)
    wide = np.arange(LANE * MOBA_HEADS) // MOBA_HEADS
    pick = jnp.asarray((wide[:, None] == np.arange(LANE)[None, :]).astype(np.float32), BF16)

    def k_spec(j):
        return pl.BlockSpec(page, lambda b, ph, g, pt: (layer, pt[b, g * G + j], 0, 0, 0))

    def v_spec(j):
        return pl.BlockSpec(page, lambda b, ph, g, pt: (layer, pt[b, g * ph * G + j], 0, 0, 0))

    grid_spec = pltpu.PrefetchScalarGridSpec(
        num_scalar_prefetch=1,
        grid=(bsz, 2, n_steps),
        in_specs=[row_spec, new_spec, new_spec,
                  pl.BlockSpec((LANE * MOBA_HEADS, LANE), lambda b, ph, g, pt: (0, 0))]
                 + [k_spec(j) for j in range(G)] + [v_spec(j) for j in range(G)],
        out_specs=row_spec,
        scratch_shapes=[pltpu.VMEM((LANE, MOBA_HEADS, MOBA_HEAD), F32),
                        pltpu.VMEM((QROWS, LANE), F32),
                        pltpu.VMEM((QROWS, 1), F32),
                        pltpu.VMEM((QROWS, 1), F32),
                        pltpu.VMEM((QROWS, MOBA_HEAD), F32)])
    return pl.pallas_call(
        functools.partial(_moba_sample_kernel, n_new=n_new, pages_per_block=pages_per_block),
        grid_spec=grid_spec,
        out_shape=jax.ShapeDtypeStruct((bsz, SAMPLE_ROWS, MOBA_WIDTH), F32),
        compiler_params=_cparams(("parallel", "arbitrary", "arbitrary")),
        name="moba_sample",
    )(page_table, q, k_new, v_new, pick, *([pool_k] * G), *([pool_v] * G))


def _cross_kernel(q_ref, k_ref, v_ref, o_ref):
    scale = MEM_HEAD ** -0.5
    for h in range(MEM_HEADS):
        sl = slice(h * MEM_HEAD, (h + 1) * MEM_HEAD)
        s = _mm(q_ref[:, sl], k_ref[:, sl], 1, 1) * scale
        e = jnp.exp(s - jnp.max(s, axis=-1, keepdims=True))
        pr = e / jnp.sum(e, axis=-1, keepdims=True)
        o_ref[:, sl] = _mm(pr, v_ref[:, sl]).astype(o_ref.dtype)


def cross_attention(q, mk, mv, *, tq=512):
    bsz, t, d = q.shape
    n_mem = mk.shape[1]
    tq = _row_tile(t, tq)
    kv_spec = pl.BlockSpec((None, n_mem, d), lambda b, i: (b, 0, 0))
    q_spec = pl.BlockSpec((None, tq, d), lambda b, i: (b, i, 0))
    return pl.pallas_call(
        _cross_kernel,
        grid=(bsz, t // tq),
        in_specs=[q_spec, kv_spec, kv_spec],
        out_specs=q_spec,
        out_shape=jax.ShapeDtypeStruct((bsz, t, d), BF16),
        compiler_params=_cparams(("parallel", "parallel")),
        name="cross_attention",
    )(q, mk, mv)


def _pad_cols(x, n):
    return jnp.pad(x, [(0, 0)] * (x.ndim - 1) + [(0, n - x.shape[-1])])


def _pad_rows(x, n):
    return jnp.pad(x, [(0, n - x.shape[0])] + [(0, 0)] * (x.ndim - 1))


def _rwkv_cols_padded(x):
    W = RWKV_WIDTH
    o = 3 * W
    return jnp.concatenate([
        x[..., :o],
        _pad_cols(x[..., o:o + DECAY_LORA], LORA_PAD),
        _pad_cols(x[..., o + DECAY_LORA:o + DECAY_LORA + AAA_LORA], LORA_PAD),
        _pad_cols(x[..., o + DECAY_LORA + AAA_LORA:], GATE_LORA_PAD)], axis=-1)


def _rwkv_cols_unpadded(x):
    o = 3 * RWKV_WIDTH
    return jnp.concatenate([
        x[..., :o],
        x[..., o:o + DECAY_LORA],
        x[..., o + LORA_PAD:o + LORA_PAD + AAA_LORA],
        x[..., o + 2 * LORA_PAD:o + 2 * LORA_PAD + GATE_LORA]], axis=-1)


def _constants():
    tri_ssd = jnp.asarray(np.tril(np.ones((SSD_CHUNK, SSD_CHUNK), np.float32)), BF16)
    lane_head = np.arange(2 * LANE) // RWKV_HEAD
    same_head = (lane_head[:, None] == lane_head[None, :]).astype(np.float32)
    return tri_ssd, jnp.asarray(same_head, BF16)


def _layer_params(l, P, consts):
    tri_ssd, same_head = consts
    row = lambda x: x.reshape(1, -1).astype(F32)
    w_in = P['w_in'][l]
    o_moba = RWKV_COLS
    o_ssm = o_moba + 3 * MOBA_WIDTH
    o_gate = o_ssm + SSM_COLS
    lp = {}
    for name in ('norm_ffn1', 'norm_ffn2', 'norm_mix', 'norm_cross', 'norm_mem', 'b_gate'):
        lp[name] = row(P[name][l])
    lp['w_rw'] = _rwkv_cols_padded(w_in[:, :RWKV_COLS]).astype(BF16)[None]
    lp['w_qkv'] = w_in[:, o_moba:o_ssm].astype(BF16)[None]
    lp['w_ssm'] = _pad_cols(w_in[:, o_ssm:o_gate], SSM_COLS_PAD).astype(BF16)[None]
    lp['w_gate'] = w_in[:, o_gate:].astype(BF16)[None]
    lp['rw'] = {
        'mu': _rwkv_cols_padded(row(P['rw_mu'][l])),
        'w0': row(P['rw_w0'][l]), 'a0': row(P['rw_a0'][l]),
        'k_k': row(P['rw_k_k'][l]), 'k_a': row(P['rw_k_a'][l]), 'r_k': row(P['rw_r_k'][l]),
        'ln_g': row(P['rw_ln_g'][l]), 'ln_b': row(P['rw_ln_b'][l]),
        'w_up': _pad_rows(P['rw_w_up'][l], LORA_PAD).astype(BF16),
        'a_up': _pad_rows(P['rw_a_up'][l], LORA_PAD).astype(BF16),
        'g_up': _pad_rows(P['rw_g_up'][l], GATE_LORA_PAD).astype(BF16),
        'bd': same_head,
    }
    lp['ssm'] = {
        'conv_w': P['conv_w'][l].astype(F32), 'conv_b': row(P['conv_b'][l]),
        'dt_bias': _pad_cols(row(P['dt_bias'][l]), SSM_DT_PAD),
        'a_log': _pad_cols(row(P['a_log'][l]), SSM_DT_PAD),
        'd_skip': row(jnp.repeat(P['d_skip'][l], SSM_HEAD)),
        'ssm_norm': row(P['ssm_norm'][l]),
        'tri': tri_ssd,
    }
    return lp


def _ffn(x, gain, w_in, w_out, l):
    act = swiglu_in(rms_norm(x, gain, out_dtype=BF16), w_in, l)
    return project_residual(act, w_out, l, x, scale=0.5)


def _pad_time(x, t_pad):
    return jnp.pad(x, ((0, 0), (0, t_pad - x.shape[1]), (0, 0)))


def _trunk_layer(x, l, lp, P, bsz, t, rw_shift0, rw_s0, conv0, ssm_h0, moba_fn, mem_k, mem_v):
    m = bsz * t
    h = _ffn(x, lp['norm_ffn1'], P['w_ffn1_in'], P['w_ffn1_out'], l)

    u = rms_norm(h, lp['norm_mix'], out_dtype=BF16)
    z_rw = project(u, lp['w_rw'], 0)
    qkv = project(u, lp['w_qkv'], 0)
    z_ssm = project(u, lp['w_ssm'], 0)
    z_gate = project(u, lp['w_gate'], 0)
    k = qkv[:, MOBA_WIDTH:2 * MOBA_WIDTH]
    v = qkv[:, 2 * MOBA_WIDTH:]

    t_rw = -(-t // RWKV_CHUNK) * RWKV_CHUNK
    z_rw3 = z_rw.reshape(bsz, t, RWKV_COLS_PAD)
    a_rw, rw_s = rwkv_mix(_pad_time(z_rw3, t_rw), _rwkv_cols_padded(rw_shift0)[:, None, :], rw_s0,
                          lp['rw'], t_valid=t)
    a_rw = a_rw[:, :t].reshape(m, RWKV_WIDTH)
    rw_shift = _rwkv_cols_unpadded(z_rw3[:, t - 1])

    a_moba = moba_fn(qkv.reshape(bsz, t, 3 * MOBA_WIDTH)).reshape(m, MOBA_WIDTH)

    t_ssd = -(-t // SSD_CHUNK) * SSD_CHUNK
    z_ssm3 = z_ssm.reshape(bsz, t, SSM_COLS_PAD)
    conv0_pad = jnp.pad(conv0, ((0, 0), (CONV_TAIL - (CONV_WIDTH - 1), 0), (0, 0)))
    a_ssm, ssm_h = ssd_mix(_pad_time(z_ssm3, t_ssd), conv0_pad, ssm_h0, lp['ssm'], t_valid=t)
    a_ssm = a_ssm[:, :t].reshape(m, SSM_WIDTH)
    keep = min(t, CONV_WIDTH - 1)
    xbc_tail = z_ssm3[:, t - keep:, SSM_WIDTH:SSM_WIDTH + SSM_CONV_DIM]
    conv_new = jnp.concatenate([conv0, xbc_tail], axis=1)[:, -(CONV_WIDTH - 1):]

    merged = gated_merge(a_rw, a_moba, a_ssm, P['rw_out'], P['moba_out'], P['ssm_out'], l,
                         z_gate, lp['b_gate'])
    h = project_residual(merged, P['w_mix_out'], l, h, scale=1.0)

    cq = project(rms_norm(h, lp['norm_cross'], out_dtype=BF16), P['w_cq'], l)
    t_ca = -(-t // 8) * 8
    o = cross_attention(_pad_time(cq.reshape(bsz, t, D_MODEL), t_ca), mem_k, mem_v)
    o = o[:, :t].reshape(m, D_MODEL)
    h = project_residual(o, P['w_co'], l, h, scale=1.0)

    h = _ffn(h, lp['norm_ffn2'], P['w_ffn2_in'], P['w_ffn2_out'], l)
    k4 = k.reshape(bsz, t, MOBA_HEADS, MOBA_HEAD)
    v4 = v.reshape(bsz, t, MOBA_HEADS, MOBA_HEAD)
    return h, (k4, v4, rw_s, rw_shift, conv_new, ssm_h)


def _moba_sample_fn(qkv, *, pool_k, pool_v, layer, page_table):
    q, k, v = (qkv[..., i * MOBA_WIDTH:(i + 1) * MOBA_WIDTH] for i in range(3))
    s_new = q.shape[1]
    out = moba_sample(_pad_time(q, SAMPLE_ROWS), _pad_time(k, PAGE_SIZE), _pad_time(v, PAGE_SIZE),
                      pool_k, pool_v, layer, page_table, n_new=s_new)
    return out[:, :s_new].astype(BF16)


def kernel(x_prompt, x_sample, cache_k, cache_v, cache_mem_k, cache_mem_v, state_rwkv, state_rwkv_shift, state_conv, state_ssm, page_table, mem_prompt, norm_ffn1, w_ffn1_in, w_ffn1_out, norm_mix, w_in, rw_mu, rw_w0, rw_w_up, rw_a0, rw_a_up, rw_g_up, rw_k_k, rw_k_a, rw_r_k, rw_ln_g, rw_ln_b, rw_out, moba_out, conv_w, conv_b, dt_bias, a_log, d_skip, ssm_norm, ssm_out, b_gate, w_mix_out, norm_cross, norm_mem, w_cq, w_ckv, w_co, norm_ffn2, w_ffn2_in, w_ffn2_out, norm_final):
    P = dict(norm_ffn1=norm_ffn1, w_ffn1_in=w_ffn1_in, w_ffn1_out=w_ffn1_out, norm_mix=norm_mix,
             w_in=w_in, rw_mu=rw_mu, rw_w0=rw_w0, rw_w_up=rw_w_up, rw_a0=rw_a0, rw_a_up=rw_a_up,
             rw_g_up=rw_g_up, rw_k_k=rw_k_k, rw_k_a=rw_k_a, rw_r_k=rw_r_k, rw_ln_g=rw_ln_g,
             rw_ln_b=rw_ln_b, rw_out=rw_out, moba_out=moba_out, conv_w=conv_w, conv_b=conv_b,
             dt_bias=dt_bias, a_log=a_log, d_skip=d_skip, ssm_norm=ssm_norm, ssm_out=ssm_out,
             b_gate=b_gate, w_mix_out=w_mix_out, norm_cross=norm_cross, norm_mem=norm_mem,
             w_cq=w_cq, w_ckv=w_ckv, w_co=w_co, norm_ffn2=norm_ffn2, w_ffn2_in=w_ffn2_in,
             w_ffn2_out=w_ffn2_out)
    depth = w_in.shape[0]
    bp, tp, d = x_prompt.shape
    bs, ts, _ = x_sample.shape
    n_mem = mem_prompt.shape[1]
    consts = _constants()
    h_p = x_prompt.reshape(bp * tp, d)
    h_s = x_sample.reshape(bs * ts, d)
    mem2 = mem_prompt.reshape(bp * n_mem, d)
    outs_p = [[] for _ in range(8)]
    outs_s = [[] for _ in range(6)]
    for l in range(depth):
        lp = _layer_params(l, P, consts)
        mkv = project(rms_norm(mem2, lp['norm_mem'], out_dtype=BF16), w_ckv, l)
        mk_p = mkv[:, :d].reshape(bp, n_mem, d)
        mv_p = mkv[:, d:].reshape(bp, n_mem, d)
        h_p, st = _trunk_layer(
            h_p, l, lp, P, bp, tp,
            jnp.zeros((bp, RWKV_COLS), F32),
            jnp.zeros((bp, RWKV_HEADS, RWKV_HEAD, RWKV_HEAD), F32),
            jnp.zeros((bp, CONV_WIDTH - 1, SSM_CONV_DIM), F32),
            jnp.zeros((bp, SSM_HEADS, SSM_HEAD, SSM_STATE), F32),
            moba_prompt, mk_p, mv_p)
        for dst, val in zip(outs_p, (st[0], st[1], mk_p.reshape(bp, n_mem, MEM_HEADS, MEM_HEAD),
                                     mv_p.reshape(bp, n_mem, MEM_HEADS, MEM_HEAD),
                                     st[2], st[3], st[4], st[5])):
            dst.append(val)
        moba_fn = functools.partial(_moba_sample_fn, pool_k=cache_k, pool_v=cache_v, layer=l,
                                    page_table=page_table)
        h_s, st = _trunk_layer(
            h_s, l, lp, P, bs, ts, state_rwkv_shift[l], state_rwkv[l], state_conv[l], state_ssm[l],
            moba_fn, cache_mem_k[l].reshape(bs, n_mem, d), cache_mem_v[l].reshape(bs, n_mem, d))
        for dst, val in zip(outs_s, st):
            dst.append(val)
    gain = norm_final.reshape(1, d)
    y_prompt = rms_norm(h_p, gain).reshape(bp, tp, d)
    y_sample = rms_norm(h_s, gain).reshape(bs, ts, d)
    return (y_prompt, y_sample) + tuple(jnp.stack(o) for o in outs_p) + tuple(jnp.stack(o) for o in outs_s)
```

```python
import functools

import jax
import jax.numpy as jnp
import numpy as np
from jax import lax
from jax.experimental import pallas as pl
from jax.experimental.pallas import tpu as pltpu

F32 = jnp.float32
BF16 = jnp.bfloat16

D_MODEL = 2048
NORM_EPS = 1e-6
NEG_INF = -1e30
REMOVED = -3e38

RWKV_WIDTH = 1024
RWKV_HEAD = 64
RWKV_HEADS = 16
DECAY_LORA = 64
AAA_LORA = 64
GATE_LORA = 160
RWKV_COLS = 3 * RWKV_WIDTH + DECAY_LORA + AAA_LORA + GATE_LORA
LN_X_EPS = 64e-5
LORA_PAD = 128
GATE_LORA_PAD = 256
RWKV_COLS_PAD = 3 * RWKV_WIDTH + 2 * LORA_PAD + GATE_LORA_PAD
RWKV_CHUNK = 64
RWKV_INV_BASE = 8

MOBA_WIDTH = 1024
MOBA_HEAD = 128
MOBA_HEADS = 8
MOBA_BLOCK = 256
MOBA_TOPK = 3
MOBA_GROUP = 4
PAGE_SIZE = 128

SSM_WIDTH = 1024
SSM_HEAD = 64
SSM_HEADS = 16
SSM_GROUPS = 4
SSM_STATE = 128
CONV_WIDTH = 4
SSM_CONV_DIM = SSM_WIDTH + 2 * SSM_GROUPS * SSM_STATE
SSM_COLS = SSM_WIDTH + SSM_CONV_DIM + SSM_HEADS
SSM_DT_PAD = 128
SSM_COLS_PAD = SSM_WIDTH + SSM_CONV_DIM + SSM_DT_PAD
SSD_CHUNK = 128
CONV_TAIL = 8

N_BRANCH = 3
MEM_HEADS = 4
MEM_HEAD = 512

LANE = 128
VMEM_LIMIT = 56 * 1024 * 1024


def _cparams(sem):
    return pltpu.CompilerParams(dimension_semantics=sem, vmem_limit_bytes=VMEM_LIMIT)


def _dg(a, b, ca, cb):
    return lax.dot_general(a, b, (((ca,), (cb,)), ((), ())), preferred_element_type=F32)


def _mm(a, b, ca=1, cb=0):
    return _dg(a.astype(BF16), b.astype(BF16), ca, cb)


def _split2(x):
    hi = x.astype(BF16)
    lo = (x - hi.astype(F32)).astype(BF16)
    return hi, lo


def _split3(x):
    hi = x.astype(BF16)
    r = x - hi.astype(F32)
    mid = r.astype(BF16)
    lo = (r - mid.astype(F32)).astype(BF16)
    return hi, mid, lo


def _mm3(a, b, ca=1, cb=0):
    ah, al = _split2(a)
    bh, bl = _split2(b)
    return _dg(ah, bh, ca, cb) + (_dg(ah, bl, ca, cb) + _dg(al, bh, ca, cb))


def _mm6(a, b, ca=1, cb=0):
    a0, a1, a2 = _split3(a)
    b0, b1, b2 = _split3(b)
    small = _dg(a0, b2, ca, cb) + _dg(a2, b0, ca, cb) + _dg(a1, b1, ca, cb)
    mid = _dg(a0, b1, ca, cb) + _dg(a1, b0, ca, cb)
    return _dg(a0, b0, ca, cb) + (mid + small)


def _mm_exact_lhs(lhs01, x):
    x0, x1, x2 = _split3(x)
    return _dg(lhs01, x0, 1, 0) + (_dg(lhs01, x1, 1, 0) + _dg(lhs01, x2, 1, 0))


def _sigmoid(x):
    return 1.0 / (1.0 + jnp.exp(-x))


def _silu(x):
    return x * _sigmoid(x)


def _softplus(x):
    return jnp.maximum(x, 0.0) + jnp.log(1.0 + jnp.exp(-jnp.abs(x)))


def _rms_rows(x, gain):
    ms = jnp.mean(x * x, axis=-1, keepdims=True)
    return x * lax.rsqrt(ms + NORM_EPS) * gain


def _norm_kernel(x_ref, g_ref, o_ref):
    o_ref[...] = _rms_rows(x_ref[...], g_ref[...]).astype(o_ref.dtype)


def _row_tile(m, want):
    return want if m % want == 0 else m


def rms_norm(x, gain, *, out_dtype=F32, tm=512):
    m, k = x.shape
    tm = _row_tile(m, tm)
    return pl.pallas_call(
        _norm_kernel,
        grid=(m // tm,),
        in_specs=[pl.BlockSpec((tm, k), lambda i: (i, 0)),
                  pl.BlockSpec((1, k), lambda i: (0, 0))],
        out_specs=pl.BlockSpec((tm, k), lambda i: (i, 0)),
        out_shape=jax.ShapeDtypeStruct((m, k), out_dtype),
        compiler_params=_cparams(("parallel",)),
        name="rms_norm",
    )(x, gain)


def _stage_weight(w_ref, wb_ref):
    @pl.when(pl.program_id(1) == 0)
    def _():
        wb_ref[...] = w_ref[...].reshape(wb_ref.shape).astype(BF16)


def _proj_kernel(x_ref, w_ref, o_ref, wb_ref):
    _stage_weight(w_ref, wb_ref)
    o_ref[...] = jnp.dot(x_ref[...], wb_ref[...], preferred_element_type=F32).astype(o_ref.dtype)


def _proj_res_kernel(x_ref, w_ref, r_ref, o_ref, wb_ref, *, scale):
    _stage_weight(w_ref, wb_ref)
    x = x_ref[:, :wb_ref.shape[0]]
    o_ref[...] = r_ref[...] + scale * jnp.dot(x, wb_ref[...], preferred_element_type=F32)


def _swiglu_kernel(x_ref, wg_ref, wu_ref, o_ref, wgb_ref, wub_ref, *, last_shift):
    _stage_weight(wg_ref, wgb_ref)
    _stage_weight(wu_ref, wub_ref)
    x = x_ref[...]
    gate = jnp.dot(x, wgb_ref[...], preferred_element_type=F32)
    up = jnp.dot(x, wub_ref[...], preferred_element_type=F32)
    act = (_silu(gate) * up).astype(o_ref.dtype)
    last = pl.num_programs(0) - 1
    tn = act.shape[1]

    @pl.when(pl.program_id(0) < last)
    def _():
        o_ref[...] = act

    @pl.when(pl.program_id(0) == last)
    def _():
        o_ref[:, :tn - last_shift] = act[:, last_shift:]
        if last_shift:
            o_ref[:, tn - last_shift:] = jnp.zeros((act.shape[0], last_shift), o_ref.dtype)


def _layer_cols(l, k, tn):
    return pl.BlockSpec((None, k, tn), lambda j, i: (l, 0, j))


def _col_tile(n):
    return next(t for t in (1024, 640, 512) if n % t == 0)


def project(x, w, l, *, tm=1024, out_dtype=F32):
    m, k = x.shape
    n = w.shape[2]
    tn = _col_tile(n)
    tm = _row_tile(m, tm)
    return pl.pallas_call(
        _proj_kernel,
        grid=(n // tn, m // tm),
        in_specs=[pl.BlockSpec((tm, k), lambda j, i: (i, 0)), _layer_cols(l, k, tn)],
        out_specs=pl.BlockSpec((tm, tn), lambda j, i: (i, j)),
        out_shape=jax.ShapeDtypeStruct((m, n), out_dtype),
        scratch_shapes=[pltpu.VMEM((k, tn), BF16)],
        compiler_params=_cparams(("parallel", "arbitrary")),
        name="project",
    )(x, w)


def project_residual(x, w, l, res, *, scale):
    m, kx = x.shape
    k, n = w.shape[1:]
    tm, tn = (1024, 1024) if k <= D_MODEL else (512, 512)
    tm = _row_tile(m, tm)
    tile = pl.BlockSpec((tm, tn), lambda j, i: (i, j))
    return pl.pallas_call(
        functools.partial(_proj_res_kernel, scale=scale),
        grid=(n // tn, m // tm),
        in_specs=[pl.BlockSpec((tm, kx), lambda j, i: (i, 0)), _layer_cols(l, k, tn), tile],
        out_specs=tile,
        out_shape=jax.ShapeDtypeStruct((m, n), F32),
        scratch_shapes=[pltpu.VMEM((k, tn), BF16)],
        compiler_params=_cparams(("parallel", "arbitrary")),
        name="project_residual",
    )(x, w, res)


def swiglu_in(x, w_in, l, *, tn=512, tm=1024):
    m, k = x.shape
    f = w_in.shape[2] // 2
    tm = _row_tile(m, tm)
    n_tiles = -(-f // tn)
    last_shift = n_tiles * tn - f
    assert last_shift % LANE == 0 and f >= tn

    def w_spec(base):
        return pl.BlockSpec((pl.Element(1), pl.Element(k), pl.Element(tn)),
                            lambda j, i: (l, 0, pl.multiple_of(base + jnp.minimum(j * tn, f - tn), LANE)))

    return pl.pallas_call(
        functools.partial(_swiglu_kernel, last_shift=last_shift),
        grid=(n_tiles, m // tm),
        in_specs=[pl.BlockSpec((tm, k), lambda j, i: (i, 0)), w_spec(0), w_spec(f)],
        out_specs=pl.BlockSpec((tm, tn), lambda j, i: (i, j)),
        out_shape=jax.ShapeDtypeStruct((m, n_tiles * tn), BF16),
        scratch_shapes=[pltpu.VMEM((k, tn), BF16), pltpu.VMEM((k, tn), BF16)],
        compiler_params=_cparams(("parallel", "arbitrary")),
        name="swiglu_in",
    )(x, w_in, w_in)


def _merge_kernel(arw_ref, amo_ref, ass_ref, wrw_ref, wmo_ref, wss_ref,
                  g0_ref, g1_ref, g2_ref, b0_ref, b1_ref, b2_ref, o_ref, wrwb_ref, wmob_ref, wssb_ref):
    _stage_weight(wrw_ref, wrwb_ref)
    _stage_weight(wmo_ref, wmob_ref)
    _stage_weight(wss_ref, wssb_ref)
    y_rw = jnp.dot(arw_ref[...], wrwb_ref[...], preferred_element_type=F32)
    y_mo = jnp.dot(amo_ref[...], wmob_ref[...], preferred_element_type=F32)
    y_ss = jnp.dot(ass_ref[...], wssb_ref[...], preferred_element_type=F32)
    merged = (_sigmoid(g0_ref[...] + b0_ref[...]) * y_rw
              + _sigmoid(g1_ref[...] + b1_ref[...]) * y_mo
              + _sigmoid(g2_ref[...] + b2_ref[...]) * y_ss)
    o_ref[...] = merged.astype(o_ref.dtype)


def gated_merge(a_rw, a_moba, a_ssm, w_rw, w_moba, w_ssm, l, z_gate, b_gate, *, tm=512, tn=512):
    m, k = a_rw.shape
    n = w_rw.shape[2]
    tm = _row_tile(m, tm)
    nj = n // tn
    a_spec = pl.BlockSpec((tm, k), lambda j, i: (i, 0))
    w_spec = _layer_cols(l, k, tn)

    def gate_spec(br):
        return pl.BlockSpec((tm, tn), lambda j, i: (i, br * nj + j))

    def bias_spec(br):
        return pl.BlockSpec((1, tn), lambda j, i: (0, br * nj + j))

    return pl.pallas_call(
        _merge_kernel,
        grid=(nj, m // tm),
        in_specs=[a_spec, a_spec, a_spec, w_spec, w_spec, w_spec,
                  gate_spec(0), gate_spec(1), gate_spec(2),
                  bias_spec(0), bias_spec(1), bias_spec(2)],
        out_specs=pl.BlockSpec((tm, tn), lambda j, i: (i, j)),
        out_shape=jax.ShapeDtypeStruct((m, n), BF16),
        scratch_shapes=[pltpu.VMEM((k, tn), BF16)] * 3,
        compiler_params=_cparams(("parallel", "arbitrary")),
        name="gated_merge",
    )(a_rw, a_moba, a_ssm, w_rw, w_moba, w_ssm, z_gate, z_gate, z_gate, b_gate, b_gate, b_gate)


def _unit_lower_inverse_many(lows, n):
    row = lax.broadcasted_iota(jnp.int32, (n, n), 0)
    col = lax.broadcasted_iota(jnp.int32, (n, n), 1)
    eye = (row == col).astype(F32)
    size = RWKV_INV_BASE

    def same_block(width):
        sh = width.bit_length() - 1
        return jnp.right_shift(row, sh) == jnp.right_shift(col, sh)

    same = same_block(size)
    power = [jnp.where(same, low, 0.0) for low in lows]
    inv = [eye + p for p in power]
    span = 2
    while span < size:
        power = [_mm3(p, p, 1, 0) for p in power]
        inv = [t + _mm3(t, p, 1, 0) for t, p in zip(inv, power)]
        span *= 2
    while size < n:
        same_next = same_block(2 * size)
        pick = same_next & jnp.logical_not(same)
        tmp = [_mm3(t, jnp.where(pick, low, 0.0), 1, 0) for t, low in zip(inv, lows)]
        inv = [t + _mm3(x, t, 1, 0) for t, x in zip(inv, tmp)]
        same = same_next
        size *= 2
    return inv


def _rwkv_kernel(z_ref, sh0_ref, s0_ref, mu_ref, w0_ref, a0_ref, kk_ref, ka_ref, rk_ref,
                 lng_ref, lnb_ref, wup_ref, aup_ref, gup_ref, bd_ref,
                 o_ref, sout_ref, state_ref, prev_ref, *, t_valid):
    C = RWKV_CHUNK
    W = RWKV_WIDTH
    c = pl.program_id(1)

    @pl.when(c == 0)
    def _():
        state_ref[...] = s0_ref[...]
        prev_ref[...] = sh0_ref[...]

    z = z_ref[...]
    row = lax.broadcasted_iota(jnp.int32, (C, 1), 0)
    valid = (c * C + row) < t_valid
    z_prev = jnp.where(row == 0, prev_ref[...], pltpu.roll(z, 1, axis=0))
    prev_ref[...] = z[C - 1:C, :]
    zm = z + (z_prev - z) * mu_ref[...]
    zm = jnp.where(valid, zm, 0.0)
    r = zm[:, 0:W]
    k = zm[:, W:2 * W]
    v = zm[:, 2 * W:3 * W]
    w_lo = zm[:, 3 * W:3 * W + LORA_PAD]
    a_lo = zm[:, 3 * W + LORA_PAD:3 * W + 2 * LORA_PAD]
    g_lo = zm[:, 3 * W + 2 * LORA_PAD:]

    w = -_softplus(-(w0_ref[...] + _mm(jnp.tanh(w_lo), wup_ref[...]))) - 0.5
    logd = jnp.where(valid, -jnp.exp(w), 0.0)
    a = _sigmoid(a0_ref[...] + _mm(a_lo, aup_ref[...]))
    g = _mm(_sigmoid(g_lo), gup_ref[...])

    bd = bd_ref[...]
    n_grp = W // (2 * LANE)

    def head_sum(x):
        parts = _split2(x)
        rows = jnp.concatenate([p[:, g_ * 2 * LANE:(g_ + 1) * 2 * LANE] for p in parts for g_ in range(n_grp)],
                               axis=0)
        res = _dg(rows, bd, 1, 0)
        tot = res[:n_grp * C] + res[n_grp * C:]
        return jnp.concatenate([tot[g_ * C:(g_ + 1) * C] for g_ in range(n_grp)], axis=1)

    kkr = k * kk_ref[...]
    kk = kkr * lax.rsqrt(jnp.maximum(head_sum(kkr * kkr), 1e-12))
    k2 = k * (1.0 + (a - 1.0) * ka_ref[...])
    b = kk * a
    bonus = head_sum(r * k2 * rk_ref[...]) * v

    cum = logd
    shift = 1
    while shift < C:
        cum = cum + jnp.where(row >= shift, pltpu.roll(cum, shift, axis=0), 0.0)
        shift *= 2
    cum_end = cum[C - 1:C, :]
    e_neg = jnp.exp(-cum)
    e_end = jnp.exp(cum_end - cum)
    a_t = -kk * jnp.exp(cum - logd)
    r_t = r * jnp.exp(cum)
    b_t = b * e_neg
    k_t = k2 * e_neg
    b_h = b * e_end
    k_h = k2 * e_end
    p_end = jnp.exp(cum_end)

    rr = lax.broadcasted_iota(jnp.int32, (C, C), 0)
    cc = lax.broadcasted_iota(jnp.int32, (C, C), 1)
    strict = rr > cc
    lower = rr >= cc

    H = range(RWKV_HEADS)
    hs = [slice(h * RWKV_HEAD, (h + 1) * RWKV_HEAD) for h in H]
    s0 = [state_ref[h] for h in H]
    ar = [jnp.concatenate([a_t[:, sl], r_t[:, sl]], axis=0) for sl in hs]
    bk = [jnp.concatenate([b_t[:, sl], k_t[:, sl]], axis=0) for sl in hs]
    vh = [v[:, sl] for sl in hs]
    gram = [_mm3(ar[h], bk[h], 1, 1) for h in H]
    ars0 = [_mm3(ar[h], s0[h], 1, 1) for h in H]
    a_ab = [jnp.where(strict, g_[:C, :C], 0.0) for g_ in gram]
    a_ak = [jnp.where(strict, g_[:C, C:], 0.0) for g_ in gram]
    r_abk = [jnp.concatenate([jnp.where(lower, g_[C:, :C], 0.0), jnp.where(lower, g_[C:, C:], 0.0)], axis=1)
             for g_ in gram]
    rhs = [ars0[h][:C] + _mm3(a_ak[h], vh[h], 1, 0) for h in H]
    inv = _unit_lower_inverse_many(a_ab, C)
    u = [_mm3(inv[h], rhs[h], 1, 0) for h in H]
    uv = [jnp.concatenate([u[h], vh[h]], axis=0) for h in H]
    ys = [ars0[h][C:] + _mm3(r_abk[h], uv[h], 1, 0) for h in H]
    bkh = [jnp.concatenate([b_h[:, sl], k_h[:, sl]], axis=0) for sl in hs]
    states = [s0[h] * p_end[:, hs[h]] + _mm3(uv[h], bkh[h], 0, 0) for h in H]
    state_ref[...] = jnp.stack(states, axis=0)

    y = jnp.concatenate(ys, axis=1)
    inv_n = 1.0 / RWKV_HEAD
    dev = y - head_sum(y) * inv_n
    var = head_sum(dev * dev) * inv_n
    yn = dev * lax.rsqrt(var + LN_X_EPS) * lng_ref[...] + lnb_ref[...]
    o_ref[...] = ((yn + bonus) * g).astype(o_ref.dtype)

    @pl.when(c == pl.num_programs(1) - 1)
    def _():
        sout_ref[...] = state_ref[...]


def rwkv_mix(z, shift0, s0, prm, *, t_valid):
    bsz, t, cols = z.shape
    C = RWKV_CHUNK
    W = RWKV_WIDTH

    def vec(n):
        return pl.BlockSpec((1, n), lambda b, c: (0, 0))

    def mat(r, n):
        return pl.BlockSpec((r, n), lambda b, c: (0, 0))

    return pl.pallas_call(
        functools.partial(_rwkv_kernel, t_valid=t_valid),
        grid=(bsz, t // C),
        in_specs=[pl.BlockSpec((None, C, cols), lambda b, c: (b, c, 0)),
                  pl.BlockSpec((None, 1, cols), lambda b, c: (b, 0, 0)),
                  pl.BlockSpec((None, RWKV_HEADS, RWKV_HEAD, RWKV_HEAD), lambda b, c: (b, 0, 0, 0)),
                  vec(cols), vec(W), vec(W), vec(W), vec(W), vec(W), vec(W), vec(W),
                  mat(LORA_PAD, W), mat(LORA_PAD, W), mat(GATE_LORA_PAD, W),
                  mat(2 * LANE, 2 * LANE)],
        out_specs=[pl.BlockSpec((None, C, W), lambda b, c: (b, c, 0)),
                   pl.BlockSpec((None, RWKV_HEADS, RWKV_HEAD, RWKV_HEAD), lambda b, c: (b, 0, 0, 0))],
        out_shape=[jax.ShapeDtypeStruct((bsz, t, W), BF16),
                   jax.ShapeDtypeStruct((bsz, RWKV_HEADS, RWKV_HEAD, RWKV_HEAD), F32)],
        scratch_shapes=[pltpu.VMEM((RWKV_HEADS, RWKV_HEAD, RWKV_HEAD), F32),
                        pltpu.VMEM((1, cols), F32)],
        compiler_params=_cparams(("parallel", "arbitrary")),
        name="rwkv_mix",
    )(z, shift0, s0, prm['mu'], prm['w0'], prm['a0'], prm['k_k'], prm['k_a'], prm['r_k'],
      prm['ln_g'], prm['ln_b'], prm['w_up'], prm['a_up'], prm['g_up'],
      prm['bd'])


def _ssd_kernel(z_ref, conv0_ref, h0_ref, cw_ref, cb_ref, dtb_ref, alog_ref, dsk_ref, nrm_ref,
                tri_ref, o_ref, hout_ref, state_ref, tail_ref, *, t_valid):
    Q = SSD_CHUNK
    W = SSM_WIDTH
    c = pl.program_id(1)

    @pl.when(c == 0)
    def _():
        state_ref[...] = h0_ref[...]
        tail_ref[...] = conv0_ref[...]

    zall = z_ref[...]
    zg = zall[:, :W]
    x = zall[:, W:W + SSM_CONV_DIM]
    dt_raw = zall[:, W + SSM_CONV_DIM:]
    tail = tail_ref[...]
    tail_ref[...] = x[Q - CONV_TAIL:, :]

    row = lax.broadcasted_iota(jnp.int32, (Q, 1), 0)
    row_t = lax.broadcasted_iota(jnp.int32, (CONV_TAIL, 1), 0)
    cw = cw_ref[...]
    acc = x * cw[CONV_WIDTH - 1:CONV_WIDTH, :]
    for s in range(1, CONV_WIDTH):
        x_s = pltpu.roll(x, s, axis=0)
        t_s = pltpu.roll(tail, s, axis=0)
        top = jnp.where(row_t < s, t_s, x_s[:CONV_TAIL])
        x_s = jnp.concatenate([top, x_s[CONV_TAIL:]], axis=0)
        acc = acc + x_s * cw[CONV_WIDTH - 1 - s:CONV_WIDTH - s, :]
    xbc = _silu(acc + cb_ref[...])
    xs = xbc[:, :W]
    bm = xbc[:, W:W + SSM_GROUPS * SSM_STATE]
    cm = xbc[:, W + SSM_GROUPS * SSM_STATE:]

    valid = (c * Q + row) < t_valid
    dt = jnp.where(valid, _softplus(dt_raw + dtb_ref[...]), 0.0)
    la = dt * (-jnp.exp(alog_ref[...]))
    cs = _mm_exact_lhs(tri_ref[...], la)
    cs_t = cs.T
    cs_end = cs[Q - 1:Q, :]
    to_end = jnp.exp(cs_end - cs)
    from_start = jnp.exp(cs)
    end_decay = jnp.exp(cs_end)

    rr = lax.broadcasted_iota(jnp.int32, (Q, Q), 0)
    cc = lax.broadcasted_iota(jnp.int32, (Q, Q), 1)
    causal = rr >= cc
    heads_per_group = SSM_HEADS // SSM_GROUPS
    H = range(SSM_HEADS)
    grp = [h // heads_per_group for h in H]
    hsl = [slice(h * SSM_HEAD, (h + 1) * SSM_HEAD) for h in H]
    b_g = [bm[:, g * SSM_STATE:(g + 1) * SSM_STATE] for g in range(SSM_GROUPS)]
    c_g = [cm[:, g * SSM_STATE:(g + 1) * SSM_STATE] for g in range(SSM_GROUPS)]
    cb = [_mm(c_g[g], b_g[g], 1, 1) for g in range(SSM_GROUPS)]
    hs = [state_ref[h] for h in H]
    xh = [xs[:, sl] for sl in hsl]
    xdt = [xh[h] * dt[:, h:h + 1] for h in H]
    decay = [jnp.exp(jnp.where(causal, cs[:, h:h + 1] - cs_t[h:h + 1, :], NEG_INF)) for h in H]
    y_diag = [_mm(cb[grp[h]] * decay[h], xdt[h]) for h in H]
    y_off = [_mm(c_g[grp[h]], hs[h], 1, 1) for h in H]
    upd = [_mm(xdt[h] * to_end[:, h:h + 1], b_g[grp[h]], 0, 0) for h in H]
    ys = [y_diag[h] + y_off[h] * from_start[:, h:h + 1] + dsk_ref[:, hsl[h]] * xh[h] for h in H]
    state_ref[...] = jnp.stack([hs[h] * end_decay[:, h:h + 1] + upd[h] for h in H], axis=0)

    yg = jnp.concatenate(ys, axis=1) * _silu(zg)
    group = W // SSM_GROUPS
    parts = []
    for g in range(SSM_GROUPS):
        t = yg[:, g * group:(g + 1) * group]
        parts.append(t * lax.rsqrt(jnp.mean(t * t, axis=-1, keepdims=True) + NORM_EPS))
    o_ref[...] = (jnp.concatenate(parts, axis=1) * nrm_ref[...]).astype(o_ref.dtype)

    @pl.when(c == pl.num_programs(1) - 1)
    def _():
        hout_ref[...] = state_ref[...]


def ssd_mix(z, conv0, h0, prm, *, t_valid):
    bsz, t, cols = z.shape
    Q = SSD_CHUNK
    W = SSM_WIDTH

    def vec(n):
        return pl.BlockSpec((1, n), lambda b, c: (0, 0))

    state_spec = pl.BlockSpec((None, SSM_HEADS, SSM_HEAD, SSM_STATE), lambda b, c: (b, 0, 0, 0))
    return pl.pallas_call(
        functools.partial(_ssd_kernel, t_valid=t_valid),
        grid=(bsz, t // Q),
        in_specs=[pl.BlockSpec((None, Q, cols), lambda b, c: (b, c, 0)),
                  pl.BlockSpec((None, CONV_TAIL, SSM_CONV_DIM), lambda b, c: (b, 0, 0)),
                  state_spec,
                  pl.BlockSpec((CONV_WIDTH, SSM_CONV_DIM), lambda b, c: (0, 0)),
                  vec(SSM_CONV_DIM), vec(SSM_DT_PAD), vec(SSM_DT_PAD), vec(W), vec(W),
                  pl.BlockSpec((Q, Q), lambda b, c: (0, 0))],
        out_specs=[pl.BlockSpec((None, Q, W), lambda b, c: (b, c, 0)), state_spec],
        out_shape=[jax.ShapeDtypeStruct((bsz, t, W), BF16),
                   jax.ShapeDtypeStruct((bsz, SSM_HEADS, SSM_HEAD, SSM_STATE), F32)],
        scratch_shapes=[pltpu.VMEM((SSM_HEADS, SSM_HEAD, SSM_STATE), F32),
                        pltpu.VMEM((CONV_TAIL, SSM_CONV_DIM), F32)],
        compiler_params=_cparams(("parallel", "arbitrary")),
        name="ssd_mix",
    )(z, conv0, h0, prm['conv_w'], prm['conv_b'], prm['dt_bias'], prm['a_log'], prm['d_skip'],
      prm['ssm_norm'], prm['tri'])


def _topk_block_mask(scores, n_allowed):
    col = lax.broadcasted_iota(jnp.int32, scores.shape, 1).astype(F32)
    sel = jnp.zeros(scores.shape, F32)
    for r in range(MOBA_TOPK):
        top = jnp.max(scores, axis=-1, keepdims=True)
        first = jnp.min(jnp.where(scores == top, col, float(scores.shape[1])), axis=-1, keepdims=True)
        pick = col == first
        sel = jnp.where(pick & (n_allowed > r), 1.0, sel)
        scores = jnp.where(pick, REMOVED, scores)
    return sel


def _attend(q, blocks, carry, scale):
    m, l, acc = carry
    qb = q.astype(BF16)
    ss = [jnp.where(keep, _mm(qb, kb, 1, 1) * scale, NEG_INF) for kb, _, keep in blocks]
    m_new = m
    for s in ss:
        m_new = jnp.maximum(m_new, jnp.max(s, axis=-1, keepdims=True))
    alpha = jnp.exp(m - m_new)
    ps = [jnp.exp(s - m_new) for s in ss]
    l = l * alpha
    acc = acc * alpha
    for p, (_, vb, _) in zip(ps, blocks):
        l = l + jnp.sum(p, axis=-1, keepdims=True)
        acc = acc + _mm(p, vb)
    return m_new, l, acc


def _topk_rows_mask(scores, n_allowed):
    row = lax.broadcasted_iota(jnp.int32, scores.shape, 0).astype(F32)
    sel = jnp.zeros(scores.shape, F32)
    for r in range(MOBA_TOPK):
        top = jnp.max(scores, axis=0, keepdims=True)
        first = jnp.min(jnp.where(scores == top, row, float(scores.shape[0])), axis=0, keepdims=True)
        pick = row == first
        sel = jnp.where(pick & (n_allowed > r), 1.0, sel)
        scores = jnp.where(pick, REMOVED, scores)
    return sel


def _block_column(sel, n):
    col = lax.broadcasted_iota(jnp.int32, sel.shape, 1)
    return jnp.sum(jnp.where(col == n, sel, 0.0), axis=-1, keepdims=True) > 0.0


def _moba_prompt_kernel(q_ref, k_ref, v_ref, o_ref, kmean_ref, *, n_blocks):
    BLK = MOBA_BLOCK
    i = pl.program_id(2)
    scale = MOBA_HEAD ** -0.5

    @pl.when(i == 0)
    def _():
        kmean_ref[...] = jnp.zeros_like(kmean_ref)
        for n in range(n_blocks):
            kmean_ref[n:n + 1, :] = jnp.mean(k_ref[n * BLK:(n + 1) * BLK, :], axis=0, keepdims=True)

    q = q_ref[...]
    s_blk = _mm6(kmean_ref[...], q, 1, 1)
    row = lax.broadcasted_iota(jnp.int32, s_blk.shape, 0)
    sel = _topk_rows_mask(jnp.where(row < i, s_blk, NEG_INF), i).T

    def block(n, keep):
        off = pl.multiple_of(n * BLK, BLK)
        return k_ref[pl.ds(off, BLK), :], v_ref[pl.ds(off, BLK), :], keep

    rr = lax.broadcasted_iota(jnp.int32, (BLK, BLK), 0)
    cc = lax.broadcasted_iota(jnp.int32, (BLK, BLK), 1)
    init = (jnp.full((BLK, 1), NEG_INF, F32), jnp.zeros((BLK, 1), F32), jnp.zeros((BLK, MOBA_HEAD), F32))
    carry = _attend(q, [block(i, rr >= cc)], init, scale)

    def past_group(g, carry):
        base = g * MOBA_GROUP
        return _attend(q, [block(base + j, _block_column(sel, base + j)) for j in range(MOBA_GROUP)],
                       carry, scale)

    n_groups = (i + MOBA_GROUP - 1) // MOBA_GROUP
    _, l, acc = lax.fori_loop(0, n_groups, past_group, carry)
    o_ref[...] = (acc / l).astype(o_ref.dtype)


def moba_prompt(qkv):
    bsz, t, _ = qkv.shape
    n_blocks = t // MOBA_BLOCK
    assert t % MOBA_BLOCK == 0 and n_blocks % MOBA_GROUP == 0, "whole groups of key blocks only"
    q_spec = pl.BlockSpec((None, MOBA_BLOCK, MOBA_HEAD), lambda b, h, i: (b, i, h))
    k_spec = pl.BlockSpec((None, t, MOBA_HEAD), lambda b, h, i: (b, 0, MOBA_HEADS + h))
    v_spec = pl.BlockSpec((None, t, MOBA_HEAD), lambda b, h, i: (b, 0, 2 * MOBA_HEADS + h))
    return pl.pallas_call(
        functools.partial(_moba_prompt_kernel, n_blocks=n_blocks),
        grid=(bsz, MOBA_HEADS, n_blocks),
        in_specs=[q_spec, k_spec, v_spec],
        out_specs=q_spec,
        out_shape=jax.ShapeDtypeStruct((bsz, t, MOBA_WIDTH), BF16),
        scratch_shapes=[pltpu.VMEM((LANE, MOBA_HEAD), F32)],
        compiler_params=_cparams(("parallel", "parallel", "arbitrary")),
        name="moba_prompt",
    )(qkv, qkv, qkv)


SAMPLE_ROWS = 8
QROWS = MOBA_HEADS * SAMPLE_ROWS
PAGES_PER_STEP = 8
PAGES_PER_UPDATE = 4
PAGE_ROWS = PAGE_SIZE * MOBA_HEADS


def _moba_sample_kernel(pt_ref, q_ref, knew_ref, vnew_ref, pick_ref, *rest, n_new, pages_per_block):
    G = PAGES_PER_STEP
    k_refs, v_refs = rest[:G], rest[G:2 * G]
    o_ref, ksum_ref, sel_ref, m_ref, l_ref, acc_ref = rest[2 * G:]
    ph = pl.program_id(1)
    g = pl.program_id(2)
    n_steps = pl.num_programs(2)
    n_blocks = (n_steps * G) // pages_per_block
    blocks_per_step = G // pages_per_block
    scale = MOBA_HEAD ** -0.5
    HEADS = range(MOBA_HEADS)
    hcol = [slice(h * MOBA_HEAD, (h + 1) * MOBA_HEAD) for h in HEADS]
    hrow = [slice(h * SAMPLE_ROWS, (h + 1) * SAMPLE_ROWS) for h in HEADS]
    row_shift = SAMPLE_ROWS.bit_length() - 1

    def own_head(n_cols):
        rr = lax.broadcasted_iota(jnp.int32, (QROWS, n_cols), 0)
        cc = lax.broadcasted_iota(jnp.int32, (QROWS, n_cols), 1)
        return jnp.bitwise_and(cc, MOBA_HEADS - 1) == jnp.right_shift(rr, row_shift)

    def q_rows():
        q = q_ref[...]
        return jnp.concatenate([q[:, hcol[h]] for h in HEADS], axis=0)

    @pl.when(ph == 0)
    def _():
        @pl.when(g == 0)
        def _():
            ksum_ref[...] = jnp.zeros_like(ksum_ref)

        for jb in range(blocks_per_step):
            tot = None
            for jp in range(pages_per_block):
                part = jnp.sum(k_refs[jb * pages_per_block + jp][...], axis=0)
                tot = part if tot is None else tot + part
            ksum_ref[g * blocks_per_step + jb] = tot

    @pl.when((ph == 1) & (g == 0))
    def _():
        qs = q_rows()
        kmean = ksum_ref[...].reshape(LANE * MOBA_HEADS, MOBA_HEAD) * (1.0 / (pages_per_block * PAGE_SIZE))
        s_blk = _mm6(qs, kmean, 1, 1)
        col = lax.broadcasted_iota(jnp.int32, s_blk.shape, 1)
        allowed = own_head(s_blk.shape[1]) & (col < n_blocks * MOBA_HEADS)
        wide = _topk_block_mask(jnp.where(allowed, s_blk, NEG_INF), n_blocks)
        sel_ref[...] = _dg(wide.astype(BF16), pick_ref[...], 1, 0)
        rr = jnp.bitwise_and(lax.broadcasted_iota(jnp.int32, (QROWS, PAGE_SIZE), 0), SAMPLE_ROWS - 1)
        cc = lax.broadcasted_iota(jnp.int32, (QROWS, PAGE_SIZE), 1)
        own_keep = (cc <= rr) & (cc < n_new)
        knew = knew_ref[...]
        vnew = vnew_ref[...]
        s_own = jnp.concatenate([_mm(qs[hrow[h]], knew[:, hcol[h]], 1, 1) for h in HEADS], axis=0) * scale
        s_own = jnp.where(own_keep, s_own, NEG_INF)
        m = jnp.max(s_own, axis=-1, keepdims=True)
        p_own = jnp.where(own_keep, jnp.exp(s_own - m), 0.0)
        m_ref[...] = m
        l_ref[...] = jnp.sum(p_own, axis=-1, keepdims=True)
        acc_ref[...] = jnp.concatenate([_mm(p_own[hrow[h]], vnew[:, hcol[h]]) for h in HEADS], axis=0)

    @pl.when(ph == 1)
    def _():
        qb = q_rows().astype(BF16)
        mine = own_head(PAGE_ROWS)
        sel = sel_ref[...]
        m, l, acc = m_ref[...], l_ref[...], acc_ref[...]
        for j0 in range(0, G, PAGES_PER_UPDATE):
            js = range(j0, j0 + PAGES_PER_UPDATE)
            ks = [k_refs[j][...].reshape(PAGE_ROWS, MOBA_HEAD) for j in js]
            vs = [v_refs[j][...].reshape(PAGE_ROWS, MOBA_HEAD) for j in js]
            keeps = [_block_column(sel, (g * G + j) // pages_per_block) for j in js]
            ss = [jnp.where(keep, jnp.where(mine, _mm(qb, kp, 1, 1) * scale, NEG_INF), NEG_INF)
                  for kp, keep in zip(ks, keeps)]
            m_new = m
            for s in ss:
                m_new = jnp.maximum(m_new, jnp.max(s, axis=-1, keepdims=True))
            alpha = jnp.exp(m - m_new)
            ps = [jnp.exp(s - m_new) for s in ss]
            l = l * alpha
            acc = acc * alpha
            for p, vp in zip(ps, vs):
                l = l + jnp.sum(p, axis=-1, keepdims=True)
                acc = acc + _mm(p, vp)
            m = m_new
        m_ref[...] = m
        l_ref[...] = l
        acc_ref[...] = acc

    @pl.when((ph == 1) & (g == n_steps - 1))
    def _():
        out = acc_ref[...] / l_ref[...]
        for h in HEADS:
            o_ref[:, hcol[h]] = out[hrow[h]]


def moba_sample(q, k_new, v_new, pool_k, pool_v, layer, page_table, *, n_new):
    bsz, n_pages = page_table.shape
    G = PAGES_PER_STEP
    pages_per_block = MOBA_BLOCK // PAGE_SIZE
    assert n_pages % G == 0 and G % pages_per_block == 0 and G % PAGES_PER_UPDATE == 0
    assert n_pages // pages_per_block <= LANE, "block scores live in one lane tile per head"
    n_steps = n_pages // G
    row_spec = pl.BlockSpec((None, SAMPLE_ROWS, MOBA_WIDTH), lambda b, ph, g, pt: (b, 0, 0))
    new_spec = pl.BlockSpec((None, PAGE_SIZE, MOBA_WIDTH), lambda b, ph, g, pt: (b, 0, 0))
    page = (None, None, PAGE_SIZE, MOBA_HEADS, MOBA_HEAD)
    wide = np.arange(LANE * MOBA_HEADS) // MOBA_HEADS
    pick = jnp.asarray((wide[:, None] == np.arange(LANE)[None, :]).astype(np.float32), BF16)

    def k_spec(j):
        return pl.BlockSpec(page, lambda b, ph, g, pt: (layer, pt[b, g * G + j], 0, 0, 0))

    def v_spec(j):
        return pl.BlockSpec(page, lambda b, ph, g, pt: (layer, pt[b, g * ph * G + j], 0, 0, 0))

    grid_spec = pltpu.PrefetchScalarGridSpec(
        num_scalar_prefetch=1,
        grid=(bsz, 2, n_steps),
        in_specs=[row_spec, new_spec, new_spec,
                  pl.BlockSpec((LANE * MOBA_HEADS, LANE), lambda b, ph, g, pt: (0, 0))]
                 + [k_spec(j) for j in range(G)] + [v_spec(j) for j in range(G)],
        out_specs=row_spec,
        scratch_shapes=[pltpu.VMEM((LANE, MOBA_HEADS, MOBA_HEAD), F32),
                        pltpu.VMEM((QROWS, LANE), F32),
                        pltpu.VMEM((QROWS, 1), F32),
                        pltpu.VMEM((QROWS, 1), F32),
                        pltpu.VMEM((QROWS, MOBA_HEAD), F32)])
    return pl.pallas_call(
        functools.partial(_moba_sample_kernel, n_new=n_new, pages_per_block=pages_per_block),
        grid_spec=grid_spec,
        out_shape=jax.ShapeDtypeStruct((bsz, SAMPLE_ROWS, MOBA_WIDTH), F32),
        compiler_params=_cparams(("parallel", "arbitrary", "arbitrary")),
        name="moba_sample",
    )(page_table, q, k_new, v_new, pick, *([pool_k] * G), *([pool_v] * G))


def _cross_kernel(q_ref, k_ref, v_ref, o_ref):
    scale = MEM_HEAD ** -0.5
    for h in range(MEM_HEADS):
        sl = slice(h * MEM_HEAD, (h + 1) * MEM_HEAD)
        s = _mm(q_ref[:, sl], k_ref[:, sl], 1, 1) * scale
        e = jnp.exp(s - jnp.max(s, axis=-1, keepdims=True))
        pr = e / jnp.sum(e, axis=-1, keepdims=True)
        o_ref[:, sl] = _mm(pr, v_ref[:, sl]).astype(o_ref.dtype)


def cross_attention(q, mk, mv, *, tq=512):
    bsz, t, d = q.shape
    n_mem = mk.shape[1]
    tq = _row_tile(t, tq)
    kv_spec = pl.BlockSpec((None, n_mem, d), lambda b, i: (b, 0, 0))
    q_spec = pl.BlockSpec((None, tq, d), lambda b, i: (b, i, 0))
    return pl.pallas_call(
        _cross_kernel,
        grid=(bsz, t // tq),
        in_specs=[q_spec, kv_spec, kv_spec],
        out_specs=q_spec,
        out_shape=jax.ShapeDtypeStruct((bsz, t, d), BF16),
        compiler_params=_cparams(("parallel", "parallel")),
        name="cross_attention",
    )(q, mk, mv)


def _pad_cols(x, n):
    return jnp.pad(x, [(0, 0)] * (x.ndim - 1) + [(0, n - x.shape[-1])])


def _pad_rows(x, n):
    return jnp.pad(x, [(0, n - x.shape[0])] + [(0, 0)] * (x.ndim - 1))


def _rwkv_cols_padded(x):
    W = RWKV_WIDTH
    o = 3 * W
    return jnp.concatenate([
        x[..., :o],
        _pad_cols(x[..., o:o + DECAY_LORA], LORA_PAD),
        _pad_cols(x[..., o + DECAY_LORA:o + DECAY_LORA + AAA_LORA], LORA_PAD),
        _pad_cols(x[..., o + DECAY_LORA + AAA_LORA:], GATE_LORA_PAD)], axis=-1)


def _rwkv_cols_unpadded(x):
    o = 3 * RWKV_WIDTH
    return jnp.concatenate([
        x[..., :o],
        x[..., o:o + DECAY_LORA],
        x[..., o + LORA_PAD:o + LORA_PAD + AAA_LORA],
        x[..., o + 2 * LORA_PAD:o + 2 * LORA_PAD + GATE_LORA]], axis=-1)


def _constants():
    tri_ssd = jnp.asarray(np.tril(np.ones((SSD_CHUNK, SSD_CHUNK), np.float32)), BF16)
    lane_head = np.arange(2 * LANE) // RWKV_HEAD
    same_head = (lane_head[:, None] == lane_head[None, :]).astype(np.float32)
    return tri_ssd, jnp.asarray(same_head, BF16)


def _layer_params(l, P, consts):
    tri_ssd, same_head = consts
    row = lambda x: x.reshape(1, -1).astype(F32)
    w_in = P['w_in'][l]
    o_moba = RWKV_COLS
    o_ssm = o_moba + 3 * MOBA_WIDTH
    o_gate = o_ssm + SSM_COLS
    lp = {}
    for name in ('norm_ffn1', 'norm_ffn2', 'norm_mix', 'norm_cross', 'norm_mem', 'b_gate'):
        lp[name] = row(P[name][l])
    lp['w_rw'] = _rwkv_cols_padded(w_in[:, :RWKV_COLS]).astype(BF16)[None]
    lp['w_qkv'] = w_in[:, o_moba:o_ssm].astype(BF16)[None]
    lp['w_ssm'] = _pad_cols(w_in[:, o_ssm:o_gate], SSM_COLS_PAD).astype(BF16)[None]
    lp['w_gate'] = w_in[:, o_gate:].astype(BF16)[None]
    lp['rw'] = {
        'mu': _rwkv_cols_padded(row(P['rw_mu'][l])),
        'w0': row(P['rw_w0'][l]), 'a0': row(P['rw_a0'][l]),
        'k_k': row(P['rw_k_k'][l]), 'k_a': row(P['rw_k_a'][l]), 'r_k': row(P['rw_r_k'][l]),
        'ln_g': row(P['rw_ln_g'][l]), 'ln_b': row(P['rw_ln_b'][l]),
        'w_up': _pad_rows(P['rw_w_up'][l], LORA_PAD).astype(BF16),
        'a_up': _pad_rows(P['rw_a_up'][l], LORA_PAD).astype(BF16),
        'g_up': _pad_rows(P['rw_g_up'][l], GATE_LORA_PAD).astype(BF16),
        'bd': same_head,
    }
    lp['ssm'] = {
        'conv_w': P['conv_w'][l].astype(F32), 'conv_b': row(P['conv_b'][l]),
        'dt_bias': _pad_cols(row(P['dt_bias'][l]), SSM_DT_PAD),
        'a_log': _pad_cols(row(P['a_log'][l]), SSM_DT_PAD),
        'd_skip': row(jnp.repeat(P['d_skip'][l], SSM_HEAD)),
        'ssm_norm': row(P['ssm_norm'][l]),
        'tri': tri_ssd,
    }
    return lp


def _ffn(x, gain, w_in, w_out, l):
    act = swiglu_in(rms_norm(x, gain, out_dtype=BF16), w_in, l)
    return project_residual(act, w_out, l, x, scale=0.5)


def _pad_time(x, t_pad):
    return jnp.pad(x, ((0, 0), (0, t_pad - x.shape[1]), (0, 0)))


def _trunk_layer(x, l, lp, P, bsz, t, rw_shift0, rw_s0, conv0, ssm_h0, moba_fn, mem_k, mem_v):
    m = bsz * t
    h = _ffn(x, lp['norm_ffn1'], P['w_ffn1_in'], P['w_ffn1_out'], l)

    u = rms_norm(h, lp['norm_mix'], out_dtype=BF16)
    z_rw = project(u, lp['w_rw'], 0)
    qkv = project(u, lp['w_qkv'], 0)
    z_ssm = project(u, lp['w_ssm'], 0)
    z_gate = project(u, lp['w_gate'], 0)
    k = qkv[:, MOBA_WIDTH:2 * MOBA_WIDTH]
    v = qkv[:, 2 * MOBA_WIDTH:]

    t_rw = -(-t // RWKV_CHUNK) * RWKV_CHUNK
    z_rw3 = z_rw.reshape(bsz, t, RWKV_COLS_PAD)
    a_rw, rw_s = rwkv_mix(_pad_time(z_rw3, t_rw), _rwkv_cols_padded(rw_shift0)[:, None, :], rw_s0,
                          lp['rw'], t_valid=t)
    a_rw = a_rw[:, :t].reshape(m, RWKV_WIDTH)
    rw_shift = _rwkv_cols_unpadded(z_rw3[:, t - 1])

    a_moba = moba_fn(qkv.reshape(bsz, t, 3 * MOBA_WIDTH)).reshape(m, MOBA_WIDTH)

    t_ssd = -(-t // SSD_CHUNK) * SSD_CHUNK
    z_ssm3 = z_ssm.reshape(bsz, t, SSM_COLS_PAD)
    conv0_pad = jnp.pad(conv0, ((0, 0), (CONV_TAIL - (CONV_WIDTH - 1), 0), (0, 0)))
    a_ssm, ssm_h = ssd_mix(_pad_time(z_ssm3, t_ssd), conv0_pad, ssm_h0, lp['ssm'], t_valid=t)
    a_ssm = a_ssm[:, :t].reshape(m, SSM_WIDTH)
    keep = min(t, CONV_WIDTH - 1)
    xbc_tail = z_ssm3[:, t - keep:, SSM_WIDTH:SSM_WIDTH + SSM_CONV_DIM]
    conv_new = jnp.concatenate([conv0, xbc_tail], axis=1)[:, -(CONV_WIDTH - 1):]

    merged = gated_merge(a_rw, a_moba, a_ssm, P['rw_out'], P['moba_out'], P['ssm_out'], l,
                         z_gate, lp['b_gate'])
    h = project_residual(merged, P['w_mix_out'], l, h, scale=1.0)

    cq = project(rms_norm(h, lp['norm_cross'], out_dtype=BF16), P['w_cq'], l)
    t_ca = -(-t // 8) * 8
    o = cross_attention(_pad_time(cq.reshape(bsz, t, D_MODEL), t_ca), mem_k, mem_v)
    o = o[:, :t].reshape(m, D_MODEL)
    h = project_residual(o, P['w_co'], l, h, scale=1.0)

    h = _ffn(h, lp['norm_ffn2'], P['w_ffn2_in'], P['w_ffn2_out'], l)
    k4 = k.reshape(bsz, t, MOBA_HEADS, MOBA_HEAD)
    v4 = v.reshape(bsz, t, MOBA_HEADS, MOBA_HEAD)
    return h, (k4, v4, rw_s, rw_shift, conv_new, ssm_h)


def _moba_sample_fn(qkv, *, pool_k, pool_v, layer, page_table):
    q, k, v = (qkv[..., i * MOBA_WIDTH:(i + 1) * MOBA_WIDTH] for i in range(3))
    s_new = q.shape[1]
    out = moba_sample(_pad_time(q, SAMPLE_ROWS), _pad_time(k, PAGE_SIZE), _pad_time(v, PAGE_SIZE),
                      pool_k, pool_v, layer, page_table, n_new=s_new)
    return out[:, :s_new].astype(BF16)


def kernel(x_prompt, x_sample, cache_k, cache_v, cache_mem_k, cache_mem_v, state_rwkv, state_rwkv_shift, state_conv, state_ssm, page_table, mem_prompt, norm_ffn1, w_ffn1_in, w_ffn1_out, norm_mix, w_in, rw_mu, rw_w0, rw_w_up, rw_a0, rw_a_up, rw_g_up, rw_k_k, rw_k_a, rw_r_k, rw_ln_g, rw_ln_b, rw_out, moba_out, conv_w, conv_b, dt_bias, a_log, d_skip, ssm_norm, ssm_out, b_gate, w_mix_out, norm_cross, norm_mem, w_cq, w_ckv, w_co, norm_ffn2, w_ffn2_in, w_ffn2_out, norm_final):
    P = dict(norm_ffn1=norm_ffn1, w_ffn1_in=w_ffn1_in, w_ffn1_out=w_ffn1_out, norm_mix=norm_mix,
             w_in=w_in, rw_mu=rw_mu, rw_w0=rw_w0, rw_w_up=rw_w_up, rw_a0=rw_a0, rw_a_up=rw_a_up,
             rw_g_up=rw_g_up, rw_k_k=rw_k_k, rw_k_a=rw_k_a, rw_r_k=rw_r_k, rw_ln_g=rw_ln_g,
             rw_ln_b=rw_ln_b, rw_out=rw_out, moba_out=moba_out, conv_w=conv_w, conv_b=conv_b,
             dt_bias=dt_bias, a_log=a_log, d_skip=d_skip, ssm_norm=ssm_norm, ssm_out=ssm_out,
             b_gate=b_gate, w_mix_out=w_mix_out, norm_cross=norm_cross, norm_mem=norm_mem,
             w_cq=w_cq, w_ckv=w_ckv, w_co=w_co, norm_ffn2=norm_ffn2, w_ffn2_in=w_ffn2_in,
             w_ffn2_out=w_ffn2_out)
    depth = w_in.shape[0]
    bp, tp, d = x_prompt.shape
    bs, ts, _ = x_sample.shape
    n_mem = mem_prompt.shape[1]
    consts = _constants()
    h_p = x_prompt.reshape(bp * tp, d)
    h_s = x_sample.reshape(bs * ts, d)
    mem2 = mem_prompt.reshape(bp * n_mem, d)
    outs_p = [[] for _ in range(8)]
    outs_s = [[] for _ in range(6)]
    for l in range(depth):
        lp = _layer_params(l, P, consts)
        mkv = project(rms_norm(mem2, lp['norm_mem'], out_dtype=BF16), w_ckv, l)
        mk_p = mkv[:, :d].reshape(bp, n_mem, d)
        mv_p = mkv[:, d:].reshape(bp, n_mem, d)
        h_p, st = _trunk_layer(
            h_p, l, lp, P, bp, tp,
            jnp.zeros((bp, RWKV_COLS), F32),
            jnp.zeros((bp, RWKV_HEADS, RWKV_HEAD, RWKV_HEAD), F32),
            jnp.zeros((bp, CONV_WIDTH - 1, SSM_CONV_DIM), F32),
            jnp.zeros((bp, SSM_HEADS, SSM_HEAD, SSM_STATE), F32),
            moba_prompt, mk_p, mv_p)
        for dst, val in zip(outs_p, (st[0], st[1], mk_p.reshape(bp, n_mem, MEM_HEADS, MEM_HEAD),
                                     mv_p.reshape(bp, n_mem, MEM_HEADS, MEM_HEAD),
                                     st[2], st[3], st[4], st[5])):
            dst.append(val)
        moba_fn = functools.partial(_moba_sample_fn, pool_k=cache_k, pool_v=cache_v, layer=l,
                                    page_table=page_table)
        h_s, st = _trunk_layer(
            h_s, l, lp, P, bs, ts, state_rwkv_shift[l], state_rwkv[l], state_conv[l], state_ssm[l],
            moba_fn, cache_mem_k[l].reshape(bs, n_mem, d), cache_mem_v[l].reshape(bs, n_mem, d))
        for dst, val in zip(outs_s, st):
            dst.append(val)
    gain = norm_final.reshape(1, d)
    y_prompt = rms_norm(h_p, gain).reshape(bp, tp, d)
    y_sample = rms_norm(h_s, gain).reshape(bs, ts, d)
    return (y_prompt, y_sample) + tuple(jnp.stack(o) for o in outs_p) + tuple(jnp.stack(o) for o in outs_s)
```
